```python
import math
import jax, jax.numpy as jnp
from jax import lax
import numpy as np

D_MODEL = 1024
BATCH = 8
SEQ = 2048
DEPTH = 2

GRID_W = 64
CTX_LEN = 256
HEAD_DIM = 64
ROT_PER_AXIS = HEAD_DIM // 2
ROPE_THETA = 10000.0
GDN_HEADS = 8
GDN_DK = 64
GDN_DV = 64
GDN_CONV = 5
GDN_CHUNK = 64
GA_HEADS = 4
GA_KV = 2
WA_HEADS = 4
WA_KV = 2
WINDOW = 128
Q_BLOCK = 128
D_FF = 4 * D_MODEL
N_MOD = 6
RMS_EPS = 1e-6
A_QKV = GDN_HEADS * (2 * GDN_DK + GDN_DV)
A_Z = GDN_HEADS * GDN_DV
A_GATES = 2 * GDN_HEADS
B_QKV = (GA_HEADS + 2 * GA_KV) * HEAD_DIM
C_QKV = (WA_HEADS + 2 * WA_KV) * HEAD_DIM
D_IN = A_QKV + A_Z + 2 * A_GATES + B_QKV + C_QKV
D_MIX = GDN_HEADS * GDN_DV + GA_HEADS * HEAD_DIM + WA_HEADS * HEAD_DIM

kernel_name = "hybrid_gdn_gqa_swa_flow_block"


def rms_norm(x, g):
    xf = x.astype(jnp.float32)
    y = xf * lax.rsqrt(jnp.mean(jnp.square(xf), axis=-1, keepdims=True) + RMS_EPS)
    return (y * g.astype(jnp.float32)).astype(x.dtype)


def modulate(x, g, shift, scale):
    return rms_norm(x, g) * (1 + scale) + shift


def ada_mod(cond, w_mod, b_mod):
    return jnp.split(jax.nn.silu(cond) @ w_mod + b_mod, N_MOD, axis=-1)


def split_cols(p):
    sizes = (A_QKV, A_Z, A_GATES, A_GATES, B_QKV, C_QKV)
    offsets = [int(o) for o in np.cumsum(sizes)[:-1]]
    return jnp.split(p, offsets, axis=-1)


def l2_normalize(x):
    return x * lax.rsqrt(jnp.sum(jnp.square(x), axis=-1, keepdims=True) + 1e-6)


def short_conv(x, w):
    pad = GDN_CONV // 2
    t = x.shape[1]
    xp = jnp.pad(x, ((0, 0), (pad, pad), (0, 0)))
    return sum(xp[:, j:j + t] * w[j] for j in range(GDN_CONV))


def axial_rope_tables(rows, dtype):
    row = jnp.repeat(jnp.arange(rows, dtype=jnp.float32), GRID_W)
    col = jnp.tile(jnp.arange(GRID_W, dtype=jnp.float32), rows)
    half = ROT_PER_AXIS // 2
    inv_freq = ROPE_THETA ** (-jnp.arange(half, dtype=jnp.float32) / half)
    ang_r = row[:, None] * inv_freq
    ang_c = col[:, None] * inv_freq
    return tuple(a[:, None, :].astype(dtype) for a in
                 (jnp.cos(ang_r), jnp.sin(ang_r), jnp.cos(ang_c), jnp.sin(ang_c)))


def _rotate(x, cos, sin):
    x1, x2 = jnp.split(x, 2, axis=-1)
    return jnp.concatenate([x1 * cos - x2 * sin, x2 * cos + x1 * sin], axis=-1)


def apply_axial_rope(x, rope):
    cos_r, sin_r, cos_c, sin_c = rope
    return jnp.concatenate([_rotate(x[..., :ROT_PER_AXIS], cos_r, sin_r),
                            _rotate(x[..., ROT_PER_AXIS:], cos_c, sin_c)], axis=-1)


def gated_delta_chunked(q, k, v, g, beta, s0):
    b, t, h, dk = k.shape
    dv = v.shape[-1]
    n = t // GDN_CHUNK

    def chunks(a):
        a = a.reshape((b, n, GDN_CHUNK, h) + a.shape[3:])
        return jnp.moveaxis(a, (1, 3), (0, 2))

    kc, vc, gc, bc = chunks(k), chunks(v), chunks(g), chunks(beta)
    gcum = jnp.cumsum(gc, axis=-1)
    idx = jnp.arange(GDN_CHUNK)
    incl = idx[:, None] >= idx[None, :]
    decay = jnp.exp(jnp.where(incl, gcum[..., :, None] - gcum[..., None, :], -jnp.inf))
    kb = kc * bc[..., None]
    low = jnp.where(idx[:, None] > idx[None, :], jnp.einsum("nbhid,nbhjd->nbhij", kb, kc) * decay, 0.0)
    eye = jnp.eye(GDN_CHUNK, dtype=low.dtype)
    rhs = jnp.concatenate([vc * bc[..., None], kb * jnp.exp(gcum)[..., None]], axis=-1)
    sol = lax.linalg.triangular_solve(eye + low, rhs, left_side=True, lower=True, unit_diagonal=True)
    u, w = sol[..., :dv], sol[..., dv:]
    g_last = gcum[..., -1]
    k_end = kc * jnp.exp(g_last[..., None] - gcum)[..., None]

    def update(s, u_i, w_i, ke_i, gl_i):
        v_new = u_i - jnp.einsum("bhck,bhkv->bhcv", w_i, s)
        s_new = s * jnp.exp(gl_i)[..., None, None] + jnp.einsum("bhck,bhcv->bhkv", ke_i, v_new)
        return s_new, v_new

    if q is None:
        def step_state(s, xs):
            s_new, _ = update(s, *xs)
            return s_new, None
        s_fin, _ = lax.scan(step_state, s0, (u, w, k_end, g_last))
        return None, s_fin

    qc = chunks(q)
    intra = jnp.einsum("nbhid,nbhjd->nbhij", qc, kc) * decay
    q_dec = qc * jnp.exp(gcum)[..., None]

    def step(s, xs):
        u_i, w_i, ke_i, gl_i, q_i, a_i = xs
        s_new, v_new = update(s, u_i, w_i, ke_i, gl_i)
        o_i = jnp.einsum("bhck,bhkv->bhcv", q_i, s) + jnp.einsum("bhij,bhjv->bhiv", a_i, v_new)
        return s_new, o_i

    s_fin, o = lax.scan(step, s0, (u, w, k_end, g_last, q_dec, intra))
    return jnp.moveaxis(o, (0, 2), (1, 3)).reshape(b, t, h, dv), s_fin


def gdn_heads(qkv, beta_raw, alpha_raw, conv_w, a_log, dt_bias, with_q):
    b, t, _ = qkv.shape
    f32 = jnp.float32
    qkv = jax.nn.silu(short_conv(qkv, conv_w)).astype(f32)
    q, k, v = jnp.split(qkv, [GDN_HEADS * GDN_DK, 2 * GDN_HEADS * GDN_DK], axis=-1)
    k = l2_normalize(k.reshape(b, t, GDN_HEADS, GDN_DK))
    v = v.reshape(b, t, GDN_HEADS, GDN_DV)
    q = l2_normalize(q.reshape(b, t, GDN_HEADS, GDN_DK)) * GDN_DK ** -0.5 if with_q else None
    beta = jax.nn.sigmoid(beta_raw.astype(f32)).reshape(b, t, 2, GDN_HEADS)
    g = -jnp.exp(a_log.astype(f32)) * jax.nn.softplus(
        alpha_raw.astype(f32).reshape(b, t, 2, GDN_HEADS) + dt_bias.astype(f32))
    return q, k, v, g, beta


def direction_inputs(heads, d):
    q, k, v, g, beta = heads
    f = lambda a: None if a is None else (jnp.flip(a, axis=1) if d == 1 else a)
    return f(q), f(k), f(v), f(g[:, :, d]), f(beta[:, :, d])


def gated_out_norm(o, z, gain):
    b, t, h, dv = o.shape
    y = rms_norm(o, gain) * jax.nn.silu(z.reshape(b, t, h, dv).astype(jnp.float32))
    return y.reshape(b, t, h * dv).astype(z.dtype)


def gdn_mixer(a_qkv, a_z, a_beta, a_alpha, ca_qkv, ca_z, ca_beta, ca_alpha,
              conv_w, a_log, dt_bias, norm_g, need_ctx_out):
    lat = gdn_heads(a_qkv, a_beta, a_alpha, conv_w, a_log, dt_bias, True)
    ctxh = gdn_heads(ca_qkv, ca_beta, ca_alpha, conv_w, a_log, dt_bias, need_ctx_out)
    s0 = jnp.zeros((a_qkv.shape[0], GDN_HEADS, GDN_DK, GDN_DV), jnp.float32)
    o_lat, o_ctx = None, None
    for d in range(2):
        oc, sc = gated_delta_chunked(*direction_inputs(ctxh, d), s0)
        ol, _ = gated_delta_chunked(*direction_inputs(lat, d), sc)
        ol = jnp.flip(ol, axis=1) if d == 1 else ol
        o_lat = ol if o_lat is None else o_lat + ol
        if need_ctx_out:
            oc = jnp.flip(oc, axis=1) if d == 1 else oc
            o_ctx = oc if o_ctx is None else o_ctx + oc
    out = gated_out_norm(o_lat, a_z, norm_g)
    out_c = gated_out_norm(o_ctx, ca_z, norm_g) if need_ctx_out else None
    return out, out_c


def attn_heads(p, n_q, n_kv, qg, kg, rope, with_q):
    b, t, _ = p.shape
    q, k, v = jnp.split(p, [n_q * HEAD_DIM, (n_q + n_kv) * HEAD_DIM], axis=-1)
    k = rms_norm(k.reshape(b, t, n_kv, HEAD_DIM), kg)
    v = v.reshape(b, t, n_kv, HEAD_DIM)
    q = rms_norm(q.reshape(b, t, n_q, HEAD_DIM), qg) if with_q else None
    if rope is not None:
        k = apply_axial_rope(k, rope)
        q = apply_axial_rope(q, rope)
    return q, k, v


def global_gqa(q, k, v, qc, kc, vc):
    b, t, hq, dh = q.shape
    hkv = k.shape[2]
    grp = hq // hkv
    scale = dh ** -0.5
    k_all = jnp.concatenate([k, kc], axis=1)
    v_all = jnp.concatenate([v, vc], axis=1)

    def block(qi):
        s = jnp.einsum("bqhgd,bkhd->bhgqk", qi, k_all).astype(jnp.float32) * scale
        p = jax.nn.softmax(s, axis=-1).astype(v_all.dtype)
        return jnp.einsum("bhgqk,bkhd->bqhgd", p, v_all)

    nb = t // Q_BLOCK
    qb = jnp.moveaxis(q.reshape(b, nb, Q_BLOCK, hkv, grp, dh), 1, 0)
    o = jnp.moveaxis(lax.map(block, qb), 0, 1).reshape(b, t, hq * dh)
    o_ctx = None
    if qc is not None:
        lc = qc.shape[1]
        s = jnp.einsum("bqhgd,bkhd->bhgqk", qc.reshape(b, lc, hkv, grp, dh), kc).astype(jnp.float32) * scale
        p = jax.nn.softmax(s, axis=-1).astype(vc.dtype)
        o_ctx = jnp.einsum("bhgqk,bkhd->bqhgd", p, vc).reshape(b, lc, hq * dh)
    return o, o_ctx


def window_gqa(q, k, v, qc, kc, vc, sink):
    b, t, hq, dh = q.shape
    hkv = k.shape[2]
    grp = hq // hkv
    scale = dh ** -0.5
    f32 = jnp.float32
    nb = t // Q_BLOCK
    wb = WINDOW // Q_BLOCK
    nkb = 2 * wb + 1
    pad = ((0, 0), (wb * Q_BLOCK, wb * Q_BLOCK), (0, 0), (0, 0))

    def band(a):
        ap = jnp.pad(a, pad).reshape(b, nb + 2 * wb, Q_BLOCK, hkv, dh)
        return jnp.concatenate([ap[:, j:j + nb] for j in range(nkb)], axis=2)

    kw, vw = band(k), band(v)
    qi = jnp.arange(Q_BLOCK)
    kj = jnp.arange(nkb * Q_BLOCK) - wb * Q_BLOCK
    kpos = jnp.arange(nb)[:, None] * Q_BLOCK + kj[None, :]
    valid = (jnp.abs(kj[None, :] - qi[:, None]) <= WINDOW)[None] & ((kpos >= 0) & (kpos < t))[:, None, :]
    qb = q.reshape(b, nb, Q_BLOCK, hkv, grp, dh)
    s_win = jnp.einsum("bnqhgd,bnkhd->bnhgqk", qb, kw).astype(f32) * scale
    s_win = jnp.where(valid[None, :, None, None], s_win, -jnp.inf)
    s_ctx = jnp.einsum("bnqhgd,bchd->bnhgqc", qb, kc).astype(f32) * scale
    sink_col = jnp.broadcast_to(sink.astype(f32).reshape(hkv, grp, 1), s_win.shape[:-1])[..., None]
    p = jax.nn.softmax(jnp.concatenate([s_win, s_ctx, sink_col], axis=-1), axis=-1).astype(v.dtype)
    kwn = kw.shape[2]
    lc = kc.shape[1]
    o = (jnp.einsum("bnhgqk,bnkhd->bnqhgd", p[..., :kwn], vw)
         + jnp.einsum("bnhgqc,bchd->bnqhgd", p[..., kwn:kwn + lc], vc)).reshape(b, t, hq * dh)
    o_ctx = None
    if qc is not None:
        s = jnp.einsum("bqhgd,bkhd->bhgqk", qc.reshape(b, lc, hkv, grp, dh), kc).astype(f32) * scale
        sc = jnp.broadcast_to(sink.astype(f32).reshape(hkv, grp, 1), s.shape[:-1])[..., None]
        pc = jax.nn.softmax(jnp.concatenate([s, sc], axis=-1), axis=-1).astype(vc.dtype)
        o_ctx = jnp.einsum("bhgqk,bkhd->bqhgd", pc[..., :lc], vc).reshape(b, lc, hq * dh)
    return o, o_ctx


def sq_relu_mlp(h, w1, w2):
    return jnp.square(jax.nn.relu(h @ w1)) @ w2


def hybrid_layer(x, cx, cond, cond_ctx, rope, w_mod, b_mod, g_attn, w_in, gdn_conv_w, gdn_a_log,
                 gdn_dt_bias, gdn_norm_g, ga_q_norm_g, ga_k_norm_g, wa_q_norm_g, wa_k_norm_g, wa_sink,
                 w_out, g_mlp, w_mlp_in, w_mlp_out, need_ctx_out):
    sh_a, sc_a, gt_a, sh_m, sc_m, gt_m = ada_mod(cond, w_mod, b_mod)
    csh_a, csc_a, cgt_a, csh_m, csc_m, cgt_m = ada_mod(cond_ctx, w_mod, b_mod)
    a_qkv, a_z, a_beta, a_alpha, b_qkv, c_qkv = split_cols(modulate(x, g_attn, sh_a, sc_a) @ w_in)
    ca_qkv, ca_z, ca_beta, ca_alpha, cb_qkv, cc_qkv = split_cols(modulate(cx, g_attn, csh_a, csc_a) @ w_in)
    o_a, oc_a = gdn_mixer(a_qkv, a_z, a_beta, a_alpha, ca_qkv, ca_z, ca_beta, ca_alpha,
                          gdn_conv_w, gdn_a_log, gdn_dt_bias, gdn_norm_g, need_ctx_out)
    q, k, v = attn_heads(b_qkv, GA_HEADS, GA_KV, ga_q_norm_g, ga_k_norm_g, rope, True)
    qc, kc, vc = attn_heads(cb_qkv, GA_HEADS, GA_KV, ga_q_norm_g, ga_k_norm_g, None, need_ctx_out)
    o_b, oc_b = global_gqa(q, k, v, qc, kc, vc)
    q, k, v = attn_heads(c_qkv, WA_HEADS, WA_KV, wa_q_norm_g, wa_k_norm_g, rope, True)
    qc, kc, vc = attn_heads(cc_qkv, WA_HEADS, WA_KV, wa_q_norm_g, wa_k_norm_g, None, need_ctx_out)
    o_c, oc_c = window_gqa(q, k, v, qc, kc, vc, wa_sink)
    x = x + gt_a * (jnp.concatenate([o_a, o_b, o_c], axis=-1) @ w_out)
    x = x + gt_m * sq_relu_mlp(modulate(x, g_mlp, sh_m, sc_m), w_mlp_in, w_mlp_out)
    if need_ctx_out:
        cx = cx + cgt_a * (jnp.concatenate([oc_a, oc_b, oc_c], axis=-1) @ w_out)
        cx = cx + cgt_m * sq_relu_mlp(modulate(cx, g_mlp, csh_m, csc_m), w_mlp_in, w_mlp_out)
    return x, cx


def setup_inputs(seed: int = 0) -> dict:
    key = jax.random.key(seed)
    ks = jax.random.split(key, 24)
    f32 = jnp.float32
    L = DEPTH

    def nrm(k, shape, scale):
        return jax.random.normal(k, shape, f32) * scale

    def gain(k, shape):
        return 1.0 + 0.02 * jax.random.normal(k, shape, f32)

    dt = jnp.exp(jax.random.uniform(ks[10], (L, 2, GDN_HEADS), f32, math.log(1e-3), math.log(1e-1)))
    return {
        "x": nrm(ks[0], (BATCH, SEQ, D_MODEL), 1.0),
        "c": nrm(ks[1], (BATCH, D_MODEL), 1.0),
        "ctx": nrm(ks[2], (BATCH, CTX_LEN, D_MODEL), 1.0),
        "c_ctx": nrm(ks[3], (D_MODEL,), 1.0),
        "w_mod": nrm(ks[4], (L, D_MODEL, N_MOD * D_MODEL), 0.5 * D_MODEL ** -0.5),
        "b_mod": nrm(ks[5], (L, N_MOD * D_MODEL), 0.01),
        "g_attn": gain(ks[6], (L, D_MODEL)),
        "w_in": nrm(ks[7], (L, D_MODEL, D_IN), D_MODEL ** -0.5),
        "gdn_conv_w": nrm(ks[8], (L, GDN_CONV, A_QKV), GDN_CONV ** -0.5),
        "gdn_a_log": jnp.log(jax.random.uniform(ks[9], (L, 2, GDN_HEADS), f32, 1.0, 16.0)),
        "gdn_dt_bias": dt + jnp.log(-jnp.expm1(-dt)),
        "gdn_norm_g": gain(ks[11], (L, GDN_DV)),
        "ga_q_norm_g": gain(ks[12], (L, HEAD_DIM)),
        "ga_k_norm_g": gain(ks[13], (L, HEAD_DIM)),
        "wa_q_norm_g": gain(ks[14], (L, HEAD_DIM)),
        "wa_k_norm_g": gain(ks[15], (L, HEAD_DIM)),
        "wa_sink": nrm(ks[16], (L, WA_HEADS), 1.0),
        "w_out": nrm(ks[17], (L, D_MIX, D_MODEL), D_MIX ** -0.5),
        "g_mlp": gain(ks[18], (L, D_MODEL)),
        "w_mlp_in": nrm(ks[19], (L, D_MODEL, D_FF), D_MODEL ** -0.5),
        "w_mlp_out": nrm(ks[20], (L, D_FF, D_MODEL), D_FF ** -0.5),
    }


def reference(x, c, ctx, c_ctx, w_mod, b_mod, g_attn, w_in, gdn_conv_w, gdn_a_log, gdn_dt_bias,
              gdn_norm_g, ga_q_norm_g, ga_k_norm_g, wa_q_norm_g, wa_k_norm_g, wa_sink, w_out, g_mlp,
              w_mlp_in, w_mlp_out):
    t = x.shape[1]
    rows = t // GRID_W
    rope = axial_rope_tables(rows, x.dtype)
    cond = c[:, None, :]
    cond_ctx = c_ctx[None, None, :]
    cx = ctx
    for l in range(DEPTH):
        x, cx = hybrid_layer(x, cx, cond, cond_ctx, rope, w_mod[l], b_mod[l], g_attn[l], w_in[l],
                             gdn_conv_w[l], gdn_a_log[l], gdn_dt_bias[l], gdn_norm_g[l],
                             ga_q_norm_g[l], ga_k_norm_g[l], wa_q_norm_g[l], wa_k_norm_g[l], wa_sink[l],
                             w_out[l], g_mlp[l], w_mlp_in[l], w_mlp_out[l], l < DEPTH - 1)
    return x
```

```python
import functools
import math

import numpy as np
import jax
import jax.numpy as jnp
from jax import lax
from jax.experimental import pallas as pl
from jax.experimental.pallas import tpu as pltpu

F32 = jnp.float32
BF16 = jnp.bfloat16

D_MODEL = 1024
GRID_W = 64
HEAD_DIM = 64
ROPE_THETA = 10000.0
GDN_HEADS = 8
GDN_CONV = 5
CHUNK = 64
GA_HEADS = 4
WA_HEADS = 4
WINDOW = 128
D_FF = 4 * D_MODEL
N_MOD = 6
RMS_EPS = 1e-6
A_QKV = 1536
A_Z = 512
A_GATES = 16
B_QKV = 512
C_QKV = 512
LANES = 128
TOK_TILE = 256
WQ_TILE = 128
NEG_BIG = -1e30
VMEM_LIMIT = 56 * 1024 * 1024


def _mm(a, b):
    return jnp.dot(a, b, preferred_element_type=F32)


def _nt(a, b):
    return lax.dot_general(a, b, (((1,), (1,)), ((), ())), preferred_element_type=F32)


def _sigmoid(x):
    return 1.0 / (1.0 + jnp.exp(-x))


def _params(sem):
    return pltpu.CompilerParams(dimension_semantics=sem, vmem_limit_bytes=VMEM_LIMIT)


def _mod_kernel(c_ref, w_ref, b_ref, o_ref):
    c = c_ref[...]
    s = c * _sigmoid(c)
    o_ref[...] = _mm(s.astype(BF16), w_ref[...].astype(BF16)) + b_ref[...]


def _ada_mod(cond16, w_mod, b_mod):
    n = N_MOD * D_MODEL
    tn = 1024
    return pl.pallas_call(
        _mod_kernel,
        grid=(n // tn,),
        in_specs=[pl.BlockSpec((16, D_MODEL), lambda j: (0, 0)),
                  pl.BlockSpec((D_MODEL, tn), lambda j: (0, j)),
                  pl.BlockSpec((1, tn), lambda j: (0, j))],
        out_specs=pl.BlockSpec((16, tn), lambda j: (0, j)),
        out_shape=jax.ShapeDtypeStruct((16, n), F32),
        compiler_params=_params(("arbitrary",)),
    )(cond16, w_mod, b_mod.reshape(1, n))


def _mod_row(b, t):
    return jnp.where(t == 0, 8, b)


def _modulated(x, g, shift, scale):
    ms = jnp.mean(x * x, axis=-1, keepdims=True)
    y = x * lax.rsqrt(ms + RMS_EPS) * g
    return y * (1.0 + scale) + shift


IN_SPLITS = (A_QKV, A_Z, LANES, B_QKV, C_QKV)
IN_COLS = sum(IN_SPLITS)


def _inproj_kernel(x_ref, mod_ref, g_ref, w_ref, *o_refs):
    m = mod_ref[0]
    h = _modulated(x_ref[0], g_ref[...], m[0:1], m[1:2]).astype(BF16)
    off = 0
    for o_ref, n in zip(o_refs, IN_SPLITS):
        o_ref[0] = _mm(h, w_ref[:, off:off + n])
        off += n


def _in_proj(xt, mod, g, w):
    bsz, tt, _ = xt.shape
    nt = tt // TOK_TILE
    return pl.pallas_call(
        _inproj_kernel,
        grid=(bsz, nt),
        in_specs=[pl.BlockSpec((1, TOK_TILE, D_MODEL), lambda b, t: (b, t, 0)),
                  pl.BlockSpec((1, N_MOD, D_MODEL), lambda b, t: (_mod_row(b, t), 0, 0)),
                  pl.BlockSpec((1, D_MODEL), lambda b, t: (0, 0)),
                  pl.BlockSpec((D_MODEL, IN_COLS), lambda b, t: (0, 0))],
        out_specs=[pl.BlockSpec((1, TOK_TILE, n), lambda b, t: (b, t, 0)) for n in IN_SPLITS],
        out_shape=[jax.ShapeDtypeStruct((bsz, tt, n), F32) for n in IN_SPLITS],
        compiler_params=_params(("parallel", "parallel")),
    )(xt, mod, g, w)


def _pair_sumsq_rsqrt(y, lo, eps, denom):
    yy = y * y
    s0 = jnp.sum(jnp.where(lo, yy, 0.0), axis=-1, keepdims=True)
    s1 = jnp.sum(jnp.where(lo, 0.0, yy), axis=-1, keepdims=True)
    return jnp.where(lo, lax.rsqrt(s0 * (1.0 / denom) + eps), lax.rsqrt(s1 * (1.0 / denom) + eps))


def _gdn_kernel(q_ref, k_ref, v_ref, gate_ref, z_ref, cwq_ref, cwk_ref, cwv_ref, alog_ref, dtb_ref,
                ng_ref, o_ref, qs, ks, vs, bs, gs, ts, ofs, obs, ext, *, n_ctx_chunks, n_chunks):
    hp = pl.program_id(1)
    tt = n_chunks * CHUNK
    lane = lax.broadcasted_iota(jnp.int32, (1, LANES), 1)
    lo = lane < HEAD_DIM
    bwd_lane = (lane % 4) >= 2
    r64 = lax.broadcasted_iota(jnp.int32, (CHUNK, CHUNK), 0)
    c64 = lax.broadcasted_iota(jnp.int32, (CHUNK, CHUNK), 1)
    tri = (r64 >= c64).astype(F32)
    shift = (LANES - 8 * hp) % LANES
    alog = alog_ref[...]
    dtb = dtb_ref[...]

    def prep(c, carry):
        r0 = pl.multiple_of(c * CHUNK, CHUNK)
        first = jnp.logical_or(c == 0, c == n_ctx_chunks)
        last = jnp.logical_or(c == n_ctx_chunks - 1, c == n_chunks - 1)
        for i, (src, cw_ref, dst, norm) in enumerate(((q_ref, cwq_ref, qs, True),
                                                      (k_ref, cwk_ref, ks, True),
                                                      (v_ref, cwv_ref, vs, False))):
            prev = src[0, pl.ds(jnp.maximum(r0 - 8, 0), 8), :]
            nxt = src[0, pl.ds(jnp.minimum(r0 + CHUNK, tt - 8), 8), :]
            ext[i, 0:8, :] = jnp.where(first, 0.0, prev)
            ext[i, 8:8 + CHUNK, :] = src[0, pl.ds(r0, CHUNK), :]
            ext[i, 8 + CHUNK:16 + CHUNK, :] = jnp.where(last, 0.0, nxt)
            cw = cw_ref[...]
            y = ext[i, 6:6 + CHUNK, :] * cw[0:1]
            for j in range(1, GDN_CONV):
                y = y + ext[i, 6 + j:6 + j + CHUNK, :] * cw[j:j + 1]
            y = y * _sigmoid(y)
            if norm:
                y = y * _pair_sumsq_rsqrt(y, lo, 1e-6, 1.0)
            dst[pl.ds(r0, CHUNK), :] = y
        gt = gate_ref[0, pl.ds(r0, CHUNK), :]
        beta = _sigmoid(gt)
        a = gt + dtb
        sp = jnp.maximum(a, 0.0) + jnp.log1p(jnp.exp(-jnp.abs(a)))
        g = -jnp.exp(alog) * sp
        gpre = jnp.dot(tri, g, preferred_element_type=F32, precision=lax.Precision.HIGHEST)
        tot = gpre[CHUNK - 1:CHUNK, :]
        gsuf = tot - gpre + g
        gc = jnp.where(bwd_lane, gsuf, gpre)
        bs[pl.ds(r0, CHUNK), :] = pltpu.roll(beta, shift, 1)
        gs[pl.ds(r0, CHUNK), :] = pltpu.roll(gc, shift, 1)
        ts[pl.ds(pl.multiple_of(c * 8, 8), 8), :] = pltpu.roll(jnp.broadcast_to(tot, (8, LANES)), shift, 1)
        return carry

    lax.fori_loop(0, n_chunks, prep, 0)

    ri = lax.broadcasted_iota(jnp.int32, (LANES, LANES), 0)
    ci = lax.broadcasted_iota(jnp.int32, (LANES, LANES), 1)
    bd = (ri < HEAD_DIM) == (ci < HEAD_DIM)
    rlo = lax.broadcasted_iota(jnp.int32, (LANES, 1), 0) < HEAD_DIM
    blk8 = (ri // 8) == (ci // 8)
    merge_masks = [jnp.logical_and((ri // (2 * m)) == (ci // (2 * m)), (ri // m) != (ci // m))
                   for m in (8, 16, 32)]
    masks = []
    for d in range(2):
        incl = jnp.logical_and(bd, (ri >= ci) if d == 0 else (ri <= ci))
        strict = jnp.logical_and(bd, (ri > ci) if d == 0 else (ri < ci))
        masks.append((incl, strict))

    def split(x):
        return jnp.where(lo, x, 0.0), jnp.where(lo, 0.0, x)

    def chunk(c, d, s):
        incl, strict = masks[d]
        r0 = pl.multiple_of(c * CHUNK, CHUNK)
        q = qs[pl.ds(r0, CHUNK), :] * (HEAD_DIM ** -0.5)
        k = ks[pl.ds(r0, CHUNK), :]
        v = vs[pl.ds(r0, CHUNK), :]
        bt = bs[pl.ds(r0, CHUNK), :]
        gt = gs[pl.ds(r0, CHUNK), :]
        tl8 = ts[pl.ds(pl.multiple_of(c * 8, 8), 8), :]
        ib, ig = 2 * d, 4 + 2 * d
        bcol = jnp.where(lo, bt[:, ib:ib + 1], bt[:, ib + 1:ib + 2])
        gcol = jnp.where(lo, gt[:, ig:ig + 1], gt[:, ig + 1:ig + 2])
        gcol2 = jnp.concatenate([jnp.broadcast_to(gt[:, ig:ig + 1], (CHUNK, LANES)),
                                 jnp.broadcast_to(gt[:, ig + 1:ig + 2], (CHUNK, LANES))], axis=0)
        gtt = jnp.concatenate([gt, gt], axis=0).T
        grow2 = jnp.where(rlo, gtt[ig:ig + 1, :], gtt[ig + 1:ig + 2, :])
        dec = jnp.exp(jnp.where(incl, gcol2 - grow2, NEG_BIG))
        kb = k * bcol
        kb0, kb1 = split(kb)
        q0, q1 = split(q)
        lhs = jnp.concatenate([kb0, kb1, q0, q1], axis=0).astype(BF16)
        kk = jnp.concatenate([k, k], axis=0).astype(BF16)
        x = _nt(lhs, kk)
        low = jnp.where(strict, x[:LANES] * dec, 0.0)
        intra = x[LANES:] * dec
        l0 = jnp.where(blk8, low, 0.0)
        lb = l0.astype(BF16)
        p = _mm(lb, lb)
        n = -l0
        y = _mm(jnp.concatenate([n, p], axis=0).astype(BF16), p.astype(BF16))
        n = n + p + y[:LANES]
        p = y[LANES:]
        n = n + p + _mm(n.astype(BF16), p.astype(BF16))
        for cmask in merge_masks:
            cm = jnp.where(cmask, low, 0.0)
            tc = cm + _mm(n.astype(BF16), cm.astype(BF16))
            n = n - tc - _mm(tc.astype(BF16), n.astype(BF16))
        vb = v * bcol
        kbg = kb * jnp.exp(gcol)
        vb0, vb1 = split(vb)
        kg0, kg1 = split(kbg)
        rhs = jnp.concatenate([jnp.concatenate([vb0, kg0], axis=1),
                               jnp.concatenate([vb1, kg1], axis=1)], axis=0)
        uw2 = rhs + _mm(n.astype(BF16), rhs.astype(BF16))
        iuw2 = _mm(intra.astype(BF16), uw2.astype(BF16))
        uw = uw2[:CHUNK] + uw2[CHUNK:]
        iuw = iuw2[:CHUNK] + iuw2[CHUNK:]
        u, w = uw[:, :LANES], uw[:, LANES:]
        iu, iw = iuw[:, :LANES], iuw[:, LANES:]
        tl = jnp.where(lo, tl8[0:1, ig:ig + 1], tl8[0:1, ig + 1:ig + 2])
        ke = k * jnp.exp(tl - gcol)
        mb = _mm(ke.T.astype(BF16), jnp.concatenate([w, u], axis=1).astype(BF16))
        mraw = jnp.where(bd, mb[:, :LANES], 0.0)
        bn = jnp.where(bd, mb[:, LANES:], 0.0)
        qe = q * jnp.exp(gcol) - iw
        sb = s.astype(BF16)
        o = _mm(qe.astype(BF16), sb) + iu
        acol = jnp.exp(jnp.where(rlo, tl8[0:1, ig:ig + 1], tl8[0:1, ig + 1:ig + 2]))
        s_new = acol * s - _mm(mraw.astype(BF16), sb) + bn
        return o, s_new

    def body(i, carry):
        sf, sb_ = carry
        cf = i
        cb = jnp.where(i < n_ctx_chunks, n_ctx_chunks - 1 - i, n_chunks + n_ctx_chunks - 1 - i)
        of, sf = chunk(cf, 0, sf)
        ofs[pl.ds(pl.multiple_of(cf * CHUNK, CHUNK), CHUNK), :] = of
        ob, sb_ = chunk(cb, 1, sb_)
        obs[pl.ds(pl.multiple_of(cb * CHUNK, CHUNK), CHUNK), :] = ob
        return sf, sb_

    zero = jnp.zeros((LANES, LANES), F32)
    lax.fori_loop(0, n_chunks, body, (zero, zero))

    ng = ng_ref[...]

    def fin(c, carry):
        r0 = pl.multiple_of(c * CHUNK, CHUNK)
        o = ofs[pl.ds(r0, CHUNK), :] + obs[pl.ds(r0, CHUNK), :]
        z = z_ref[0, pl.ds(r0, CHUNK), :]
        y = o * _pair_sumsq_rsqrt(o, lo, RMS_EPS, float(HEAD_DIM)) * ng
        o_ref[0, pl.ds(r0, CHUNK), :] = (y * (z * _sigmoid(z))).astype(o_ref.dtype)
        return carry

    lax.fori_loop(0, n_chunks, fin, 0)


def _gdn(a_qkv, gates, a_z, cw8, alog_l, dtb_l, ng2, n_ctx):
    bsz, tt, _ = a_qkv.shape
    n_pairs = GDN_HEADS // 2
    tok = lambda off: pl.BlockSpec((1, tt, LANES), lambda b, h: (b, 0, h + off))
    par = lambda off: pl.BlockSpec((8, LANES), lambda b, h: (0, h + off))
    vec = pl.BlockSpec((1, LANES), lambda b, h: (0, 0))
    kern = functools.partial(_gdn_kernel, n_ctx_chunks=n_ctx // CHUNK, n_chunks=tt // CHUNK)
    return pl.pallas_call(
        kern,
        grid=(bsz, n_pairs),
        in_specs=[tok(0), tok(n_pairs), tok(2 * n_pairs),
                  pl.BlockSpec((1, tt, LANES), lambda b, h: (b, 0, 0)),
                  tok(0),
                  par(0), par(n_pairs), par(2 * n_pairs), vec, vec, vec],
        out_specs=tok(0),
        out_shape=jax.ShapeDtypeStruct((bsz, tt, A_Z), BF16),
        scratch_shapes=[pltpu.VMEM((tt, LANES), F32)] * 5
                       + [pltpu.VMEM((tt // CHUNK * 8, LANES), F32)]
                       + [pltpu.VMEM((tt, LANES), F32)] * 2
                       + [pltpu.VMEM((3, CHUNK + 16, LANES), F32)],
        compiler_params=_params(("parallel", "parallel")),
    )(a_qkv, a_qkv, a_qkv, gates, a_z, cw8, cw8, cw8, alog_l, dtb_l, ng2)


def _norm_rope(t, gain, cos, sin, lo, swap_lo):
    y = t * _pair_sumsq_rsqrt(t, lo, RMS_EPS, float(HEAD_DIM)) * gain
    sw = jnp.where(swap_lo, pltpu.roll(y, LANES - 16, 1), pltpu.roll(y, 16, 1))
    return y * cos + sw * sin


def _aprep_kernel(b_ref, c_ref, cos_ref, sin_ref, gn_ref, bq_ref, bk_ref, bv_ref, cq_ref, ck_ref, cv_ref):
    lane = lax.broadcasted_iota(jnp.int32, (1, LANES), 1)
    lo = lane < HEAD_DIM
    swap_lo = (lane % 32) < 16
    cos = cos_ref[...]
    sin = sin_ref[...]
    gn = gn_ref[...]
    scale = HEAD_DIM ** -0.5
    for src, q_ref, k_ref, v_ref, gi in ((b_ref, bq_ref, bk_ref, bv_ref, 0), (c_ref, cq_ref, ck_ref, cv_ref, 2)):
        qg = gn[gi:gi + 1]
        kg = gn[gi + 1:gi + 2]
        for j in range(2):
            t = src[0, :, j * LANES:(j + 1) * LANES]
            q_ref[0, :, j * LANES:(j + 1) * LANES] = (_norm_rope(t, qg, cos, sin, lo, swap_lo) * scale).astype(BF16)
        k_ref[0] = _norm_rope(src[0, :, 2 * LANES:3 * LANES], kg, cos, sin, lo, swap_lo).astype(BF16)
        v_ref[0] = src[0, :, 3 * LANES:4 * LANES].astype(BF16)


def _attn_prep(b_qkv, c_qkv, cos_t, sin_t, gains):
    bsz, tt, _ = b_qkv.shape
    nt = tt // TOK_TILE
    tile = lambda n: pl.BlockSpec((1, TOK_TILE, n), lambda b, t: (b, t, 0))
    tab = pl.BlockSpec((TOK_TILE, LANES), lambda b, t: (t, 0))
    shp = lambda n: jax.ShapeDtypeStruct((bsz, tt, n), BF16)
    return pl.pallas_call(
        _aprep_kernel,
        grid=(bsz, nt),
        in_specs=[tile(B_QKV), tile(C_QKV), tab, tab, pl.BlockSpec((8, LANES), lambda b, t: (0, 0))],
        out_specs=[tile(256), tile(LANES), tile(LANES)] * 2,
        out_shape=[shp(256), shp(LANES), shp(LANES)] * 2,
        compiler_params=_params(("parallel", "parallel")),
    )(b_qkv, c_qkv, cos_t, sin_t, gains)


def _gattn_kernel(q_ref, k_ref, v_ref, o_ref, *, first_tile, n_ctx):
    t = pl.program_id(1) + first_tile
    lane = lax.broadcasted_iota(jnp.int32, (1, LANES), 1)
    lo = lane < HEAD_DIM

    def attend(kk, vv):
        for tile in range(2):
            qt = q_ref[0, :, tile * LANES:(tile + 1) * LANES]
            parts = []
            for hh in range(2):
                qm = jnp.where(lo if hh == 0 else jnp.logical_not(lo), qt, jnp.zeros_like(qt))
                s = _nt(qm, kk)
                mx = jnp.max(s, axis=-1, keepdims=True)
                p = jnp.exp(s - mx)
                l = jnp.sum(p, axis=-1, keepdims=True)
                parts.append(_mm(p.astype(BF16), vv) / l)
            o_ref[0, :, tile * LANES:(tile + 1) * LANES] = jnp.where(lo, parts[0], parts[1]).astype(o_ref.dtype)

    if first_tile == 0:
        @pl.when(t == 0)
        def _():
            attend(k_ref[0, 0:n_ctx, :], v_ref[0, 0:n_ctx, :])

    @pl.when(t > 0)
    def _():
        attend(k_ref[0], v_ref[0])


def _global_attn(q, k, v, n_ctx, need_ctx):
    bsz, tt, _ = q.shape
    first = 0 if need_ctx else n_ctx // TOK_TILE
    nt = tt // TOK_TILE - first
    kv = pl.BlockSpec((1, tt, LANES), lambda b, t: (b, 0, 0))
    return pl.pallas_call(
        functools.partial(_gattn_kernel, first_tile=first, n_ctx=n_ctx),
        grid=(bsz, nt),
        in_specs=[pl.BlockSpec((1, TOK_TILE, 256), lambda b, t: (b, t + first, 0)), kv, kv],
        out_specs=pl.BlockSpec((1, TOK_TILE, 256), lambda b, t: (b, t, 0)),
        out_shape=jax.ShapeDtypeStruct((bsz, nt * TOK_TILE, 256), BF16),
        compiler_params=_params(("parallel", "arbitrary")),
    )(q, k, v)


def _wattn_kernel(q_ref, k_ref, v_ref, sink_ref, o_ref, *, first_tile, n_ctx, n_lat):
    t = pl.program_id(1) + first_tile
    n_ctx_tiles = n_ctx // WQ_TILE
    lane = lax.broadcasted_iota(jnp.int32, (1, LANES), 1)
    lo = lane < HEAD_DIM
    kw = 3 * WQ_TILE
    sink = sink_ref[...]

    def attend(win):
        kc = k_ref[0, 0:n_ctx, :]
        vc = v_ref[0, 0:n_ctx, :]
        if win:
            n = t - n_ctx_tiles
            ws = jnp.clip((n - 1) * WQ_TILE, 0, n_lat - kw)
            start = pl.multiple_of(n_ctx + ws, WQ_TILE)
            kwn = k_ref[0, pl.ds(start, kw), :]
            vwn = v_ref[0, pl.ds(start, kw), :]
            qpos = n * WQ_TILE + lax.broadcasted_iota(jnp.int32, (WQ_TILE, kw), 0)
            kpos = ws + lax.broadcasted_iota(jnp.int32, (WQ_TILE, kw), 1)
            valid = jnp.abs(kpos - qpos) <= WINDOW
        for tile in range(2):
            qt = q_ref[0, :, tile * LANES:(tile + 1) * LANES]
            parts = []
            for hh in range(2):
                head = 2 * hh + tile
                sk = sink[head:head + 1, 0:1]
                qm = jnp.where(lo if hh == 0 else jnp.logical_not(lo), qt, jnp.zeros_like(qt))
                sc = _nt(qm, kc)
                mx = jnp.maximum(jnp.max(sc, axis=-1, keepdims=True), sk)
                if win:
                    sw = jnp.where(valid, _nt(qm, kwn), NEG_BIG)
                    mx = jnp.maximum(mx, jnp.max(sw, axis=-1, keepdims=True))
                pc = jnp.exp(sc - mx)
                l = jnp.sum(pc, axis=-1, keepdims=True) + jnp.exp(sk - mx)
                o = _mm(pc.astype(BF16), vc)
                if win:
                    pw = jnp.exp(sw - mx)
                    l = l + jnp.sum(pw, axis=-1, keepdims=True)
                    o = o + _mm(pw.astype(BF16), vwn)
                parts.append(o / l)
            o_ref[0, :, tile * LANES:(tile + 1) * LANES] = jnp.where(lo, parts[0], parts[1]).astype(o_ref.dtype)

    if first_tile == 0:
        @pl.when(t < n_ctx_tiles)
        def _():
            attend(False)

    @pl.when(t >= n_ctx_tiles)
    def _():
        attend(True)


def _window_attn(q, k, v, sink8, n_ctx, need_ctx):
    bsz, tt, _ = q.shape
    first = 0 if need_ctx else n_ctx // WQ_TILE
    nt = tt // WQ_TILE - first
    kv = pl.BlockSpec((1, tt, LANES), lambda b, t: (b, 0, 0))
    return pl.pallas_call(
        functools.partial(_wattn_kernel, first_tile=first, n_ctx=n_ctx, n_lat=tt - n_ctx),
        grid=(bsz, nt),
        in_specs=[pl.BlockSpec((1, WQ_TILE, 256), lambda b, t: (b, t + first, 0)), kv, kv,
                  pl.BlockSpec((8, LANES), lambda b, t: (0, 0))],
        out_specs=pl.BlockSpec((1, WQ_TILE, 256), lambda b, t: (b, t, 0)),
        out_shape=jax.ShapeDtypeStruct((bsz, nt * WQ_TILE, 256), BF16),
        compiler_params=_params(("parallel", "arbitrary")),
    )(q, k, v, sink8)


def _outmlp_kernel(x_ref, oa_ref, ob_ref, oc_ref, mod_ref, g_ref, wo_ref, w1_ref, w2_ref, o_ref):
    m = mod_ref[0]
    mix = _mm(oa_ref[0], wo_ref[0:A_Z, :])
    mix = mix + _mm(ob_ref[0], wo_ref[A_Z:A_Z + 256, :])
    mix = mix + _mm(oc_ref[0], wo_ref[A_Z + 256:A_Z + 512, :])
    x = x_ref[0] + m[2:3] * mix
    h = _modulated(x, g_ref[...], m[3:4], m[4:5]).astype(BF16)
    fc = 1024
    acc = None
    for f in range(D_FF // fc):
        a = jnp.maximum(_mm(h, w1_ref[:, f * fc:(f + 1) * fc]), 0.0)
        y = _mm((a * a).astype(BF16), w2_ref[f * fc:(f + 1) * fc, :])
        acc = y if acc is None else acc + y
    o_ref[0] = x + m[5:6] * acc


def _out_mlp(xt, oa, ob, oc, mod, g, wo, w1, w2, n_ctx, need_ctx):
    bsz, tt, _ = xt.shape
    first = 0 if need_ctx else n_ctx // TOK_TILE
    nt = tt // TOK_TILE - first
    tile = lambda n: pl.BlockSpec((1, TOK_TILE, n), lambda b, t: (b, t + first, 0))
    att = pl.BlockSpec((1, TOK_TILE, 256), lambda b, t: (b, t, 0))
    full = lambda r, c: pl.BlockSpec((r, c), lambda b, t: (0, 0))
    return pl.pallas_call(
        _outmlp_kernel,
        grid=(bsz, nt),
        in_specs=[tile(D_MODEL), tile(A_Z), att, att,
                  pl.BlockSpec((1, N_MOD, D_MODEL), lambda b, t: (_mod_row(b, t + first), 0, 0)),
                  full(1, D_MODEL), full(D_MODEL, D_MODEL), full(D_MODEL, D_FF), full(D_FF, D_MODEL)],
        out_specs=pl.BlockSpec((1, TOK_TILE, D_MODEL), lambda b, t: (b, t, 0)),
        out_shape=jax.ShapeDtypeStruct((bsz, nt * TOK_TILE, D_MODEL), F32),
        compiler_params=_params(("parallel", "parallel")),
    )(xt, oa, ob, oc, mod, g, wo, w1, w2)


_HEAD_PERM = (0, 2, 1, 3)


def _perm_cols(n_heads_q):
    cols = []
    for h in _HEAD_PERM:
        cols.extend(range(h * HEAD_DIM, (h + 1) * HEAD_DIM))
    return cols


def _gate_cols():
    base_beta = A_QKV + A_Z
    base_alpha = base_beta + A_GATES
    cols = []
    for pair in range(GDN_HEADS // 2):
        for base in (base_beta, base_alpha):
            for d in range(2):
                for j in range(2):
                    cols.append(base + d * GDN_HEADS + 2 * pair + j)
    return cols


def _gate_param_lanes(p):
    out = jnp.zeros((LANES,), F32)
    idx, src = [], []
    for pair in range(GDN_HEADS // 2):
        for d in range(2):
            for j in range(2):
                idx.append(pair * 8 + 4 + d * 2 + j)
                src.append(d * GDN_HEADS + 2 * pair + j)
    return out.at[jnp.array(idx)].set(p.reshape(-1)[jnp.array(src)]).reshape(1, LANES)


def _rope_tables(n_ctx, n_lat):
    rows = n_lat // GRID_W
    row = jnp.repeat(jnp.arange(rows, dtype=F32), GRID_W)
    col = jnp.tile(jnp.arange(GRID_W, dtype=F32), rows)
    half = HEAD_DIM // 4
    inv_freq = ROPE_THETA ** (-jnp.arange(half, dtype=F32) / half)
    ang_r = row[:, None] * inv_freq
    ang_c = col[:, None] * inv_freq
    cr, sr, cc, sc = jnp.cos(ang_r), jnp.sin(ang_r), jnp.cos(ang_c), jnp.sin(ang_c)
    cos = jnp.concatenate([cr, cr, cc, cc], axis=-1)
    sin = jnp.concatenate([-sr, sr, -sc, sc], axis=-1)
    cos = jnp.concatenate([jnp.ones((n_ctx, HEAD_DIM), F32), cos], axis=0)
    sin = jnp.concatenate([jnp.zeros((n_ctx, HEAD_DIM), F32), sin], axis=0)
    return jnp.tile(cos, (1, 2)), jnp.tile(sin, (1, 2))


def _pad_rows(a, rows):
    return jnp.concatenate([a, jnp.zeros((rows - a.shape[0],) + a.shape[1:], a.dtype)], axis=0)


def kernel(x, c, ctx, c_ctx, w_mod, b_mod, g_attn, w_in, gdn_conv_w, gdn_a_log, gdn_dt_bias, gdn_norm_g,
           ga_q_norm_g, ga_k_norm_g, wa_q_norm_g, wa_k_norm_g, wa_sink, w_out, g_mlp, w_mlp_in, w_mlp_out):
    bsz, n_lat, _ = x.shape
    n_ctx = ctx.shape[1]
    depth = w_mod.shape[0]
    assert bsz <= 8 and n_ctx == TOK_TILE and n_lat % TOK_TILE == 0

    xt = jnp.concatenate([ctx, x], axis=1)
    cond16 = _pad_rows(jnp.concatenate([_pad_rows(c, 8), c_ctx[None, :]], axis=0), 16)
    cos_t, sin_t = _rope_tables(n_ctx, n_lat)

    b0 = A_QKV + A_Z + 2 * A_GATES
    c0 = b0 + B_QKV
    qperm = _perm_cols(GA_HEADS)
    in_cols = (list(range(A_QKV + A_Z)) + _gate_cols()
               + [b0 + i for i in qperm] + list(range(b0 + 256, b0 + B_QKV))
               + [c0 + i for i in qperm] + list(range(c0 + 256, c0 + C_QKV)))
    in_cols = np.array(in_cols)
    gate_pad = jnp.zeros((D_MODEL, LANES - 2 * A_GATES), BF16)
    out_rows = np.array(list(range(A_Z)) + [A_Z + i for i in qperm] + [A_Z + 256 + i for i in qperm])

    for l in range(depth):
        need_ctx = l < depth - 1
        mod = _ada_mod(cond16, w_mod[l], b_mod[l]).reshape(16, N_MOD, D_MODEL)
        wl = w_in[l].astype(BF16)[:, in_cols]
        n_front = A_QKV + A_Z + 2 * A_GATES
        w_in_l = jnp.concatenate([wl[:, :n_front], gate_pad, wl[:, n_front:]], axis=1)
        a_qkv, a_z, gates, b_qkv, c_qkv = _in_proj(xt, mod, g_attn[l].reshape(1, D_MODEL), w_in_l)

        o_a = _gdn(a_qkv, gates, a_z, _pad_rows(gdn_conv_w[l], 8), _gate_param_lanes(gdn_a_log[l]),
                   _gate_param_lanes(gdn_dt_bias[l]), jnp.tile(gdn_norm_g[l], 2).reshape(1, LANES), n_ctx)

        gains = _pad_rows(jnp.stack([jnp.tile(g, 2) for g in
                                     (ga_q_norm_g[l], ga_k_norm_g[l], wa_q_norm_g[l], wa_k_norm_g[l])]), 8)
        bq, bk, bv, cq, ck, cv = _attn_prep(b_qkv, c_qkv, cos_t, sin_t, gains)
        o_b = _global_attn(bq, bk, bv, n_ctx, need_ctx)
        sink8 = _pad_rows(jnp.broadcast_to(wa_sink[l][:, None], (WA_HEADS, LANES)), 8)
        o_c = _window_attn(cq, ck, cv, sink8, n_ctx, need_ctx)

        xt = _out_mlp(xt, o_a, o_b, o_c, mod, g_mlp[l].reshape(1, D_MODEL),
                      w_out[l].astype(BF16)[out_rows, :], w_mlp_in[l].astype(BF16),
                      w_mlp_out[l].astype(BF16), n_ctx, need_ctx)
    return xt
```

```python
import functools
import math

import numpy as np
import jax
import jax.numpy as jnp
from jax import lax
from jax.experimental import pallas as pl
from jax.experimental.pallas import tpu as pltpu

F32 = jnp.float32
BF16 = jnp.bfloat16

D_MODEL = 1024
GRID_W = 64
HEAD_DIM = 64
ROPE_THETA = 10000.0
GDN_HEADS = 8
GDN_CONV = 5
CHUNK = 64
GA_HEADS = 4
WA_HEADS = 4
WINDOW = 128
D_FF = 4 * D_MODEL
N_MOD = 6
RMS_EPS = 1e-6
A_QKV = 1536
A_Z = 512
A_GATES = 16
B_QKV = 512
C_QKV = 512
LANES = 128
TOK_TILE = 256
WQ_TILE = 128
PAR_GROUP = 4
NEG_BIG = -1e30
VMEM_LIMIT = 56 * 1024 * 1024


def _mm(a, b):
    return jnp.dot(a, b, preferred_element_type=F32)


def _nt(a, b):
    return lax.dot_general(a, b, (((1,), (1,)), ((), ())), preferred_element_type=F32)


def _sigmoid(x):
    return 1.0 / (1.0 + jnp.exp(-x))


def _params(sem):
    return pltpu.CompilerParams(dimension_semantics=sem, vmem_limit_bytes=VMEM_LIMIT)


def _mod_kernel(c_ref, w_ref, b_ref, o_ref):
    c = c_ref[...]
    s = c * _sigmoid(c)
    o_ref[...] = _mm(s.astype(BF16), w_ref[...].astype(BF16)) + b_ref[...]


def _ada_mod(cond16, w_mod, b_mod):
    n = N_MOD * D_MODEL
    tn = 1024
    return pl.pallas_call(
        _mod_kernel,
        grid=(n // tn,),
        in_specs=[pl.BlockSpec((16, D_MODEL), lambda j: (0, 0)),
                  pl.BlockSpec((D_MODEL, tn), lambda j: (0, j)),
                  pl.BlockSpec((1, tn), lambda j: (0, j))],
        out_specs=pl.BlockSpec((16, tn), lambda j: (0, j)),
        out_shape=jax.ShapeDtypeStruct((16, n), F32),
        compiler_params=_params(("arbitrary",)),
    )(cond16, w_mod, b_mod.reshape(1, n))


def _mod_row(b, t):
    return jnp.where(t == 0, 8, b)


def _modulated(x, g, shift, scale):
    ms = jnp.mean(x * x, axis=-1, keepdims=True)
    y = x * lax.rsqrt(ms + RMS_EPS) * g
    return y * (1.0 + scale) + shift


IN_SPLITS = (A_QKV, A_Z, LANES, B_QKV, C_QKV)
IN_COLS = sum(IN_SPLITS)


def _inproj_kernel(x_ref, mod_ref, g_ref, w_ref, *o_refs):
    m = mod_ref[0]
    h = _modulated(x_ref[0], g_ref[...], m[0:1], m[1:2]).astype(BF16)
    off = 0
    for o_ref, n in zip(o_refs, IN_SPLITS):
        o_ref[0] = _mm(h, w_ref[:, off:off + n])
        off += n


def _in_proj(xt, mod, g, w):
    bsz, tt, _ = xt.shape
    nt = tt // TOK_TILE
    return pl.pallas_call(
        _inproj_kernel,
        grid=(bsz, nt),
        in_specs=[pl.BlockSpec((1, TOK_TILE, D_MODEL), lambda b, t: (b, t, 0)),
                  pl.BlockSpec((1, N_MOD, D_MODEL), lambda b, t: (_mod_row(b, t), 0, 0)),
                  pl.BlockSpec((1, D_MODEL), lambda b, t: (0, 0)),
                  pl.BlockSpec((D_MODEL, IN_COLS), lambda b, t: (0, 0))],
        out_specs=[pl.BlockSpec((1, TOK_TILE, n), lambda b, t: (b, t, 0)) for n in IN_SPLITS],
        out_shape=[jax.ShapeDtypeStruct((bsz, tt, n), F32) for n in IN_SPLITS],
        compiler_params=_params(("parallel", "parallel")),
    )(xt, mod, g, w)


def _pair_sumsq_rsqrt(y, lo, eps, denom):
    yy = y * y
    s0 = jnp.sum(jnp.where(lo, yy, 0.0), axis=-1, keepdims=True)
    s1 = jnp.sum(jnp.where(lo, 0.0, yy), axis=-1, keepdims=True)
    return jnp.where(lo, lax.rsqrt(s0 * (1.0 / denom) + eps), lax.rsqrt(s1 * (1.0 / denom) + eps))


def _gdn_kernel(q_ref, k_ref, v_ref, gate_ref, z_ref, cwq_ref, cwk_ref, cwv_ref, alog_ref, dtb_ref,
                ng_ref, o_ref, qs, ks, vs, bs, gs, ts, oacc, mq, bns, ext, *, n_ctx_chunks, n_chunks):
    hp = pl.program_id(1)
    tt = n_chunks * CHUNK
    lane = lax.broadcasted_iota(jnp.int32, (1, LANES), 1)
    lo = lane < HEAD_DIM
    bwd_lane = (lane % 4) >= 2
    r64 = lax.broadcasted_iota(jnp.int32, (CHUNK, CHUNK), 0)
    c64 = lax.broadcasted_iota(jnp.int32, (CHUNK, CHUNK), 1)
    tri = (r64 >= c64).astype(F32)
    shift = (LANES - 8 * hp) % LANES
    alog = alog_ref[...]
    dtb = dtb_ref[...]

    def prep(c, carry):
        r0 = pl.multiple_of(c * CHUNK, CHUNK)
        first = jnp.logical_or(c == 0, c == n_ctx_chunks)
        last = jnp.logical_or(c == n_ctx_chunks - 1, c == n_chunks - 1)
        for i, (src, cw_ref, dst, norm) in enumerate(((q_ref, cwq_ref, qs, True),
                                                      (k_ref, cwk_ref, ks, True),
                                                      (v_ref, cwv_ref, vs, False))):
            prev = src[0, pl.ds(jnp.maximum(r0 - 8, 0), 8), :]
            nxt = src[0, pl.ds(jnp.minimum(r0 + CHUNK, tt - 8), 8), :]
            ext[i, 0:8, :] = jnp.where(first, 0.0, prev)
            ext[i, 8:8 + CHUNK, :] = src[0, pl.ds(r0, CHUNK), :]
            ext[i, 8 + CHUNK:16 + CHUNK, :] = jnp.where(last, 0.0, nxt)
            cw = cw_ref[...]
            y = ext[i, 6:6 + CHUNK, :] * cw[0:1]
            for j in range(1, GDN_CONV):
                y = y + ext[i, 6 + j:6 + j + CHUNK, :] * cw[j:j + 1]
            y = y * _sigmoid(y)
            if norm:
                y = y * _pair_sumsq_rsqrt(y, lo, 1e-6, 1.0)
            dst[pl.ds(r0, CHUNK), :] = y
        gt = gate_ref[0, pl.ds(r0, CHUNK), :]
        beta = _sigmoid(gt)
        a = gt + dtb
        sp = jnp.maximum(a, 0.0) + jnp.log1p(jnp.exp(-jnp.abs(a)))
        g = -jnp.exp(alog) * sp
        gpre = jnp.dot(tri, g, preferred_element_type=F32, precision=lax.Precision.HIGHEST)
        tot = gpre[CHUNK - 1:CHUNK, :]
        gsuf = tot - gpre + g
        gc = jnp.where(bwd_lane, gsuf, gpre)
        bs[pl.ds(r0, CHUNK), :] = pltpu.roll(beta, shift, 1)
        gs[pl.ds(r0, CHUNK), :] = pltpu.roll(gc, shift, 1)
        ts[pl.ds(pl.multiple_of(c * 8, 8), 8), :] = pltpu.roll(jnp.broadcast_to(tot, (8, LANES)), shift, 1)
        return carry

    lax.fori_loop(0, n_chunks, prep, 0)

    ri = lax.broadcasted_iota(jnp.int32, (LANES, LANES), 0)
    ci = lax.broadcasted_iota(jnp.int32, (LANES, LANES), 1)
    bd = (ri < HEAD_DIM) == (ci < HEAD_DIM)
    rlo = lax.broadcasted_iota(jnp.int32, (LANES, 1), 0) < HEAD_DIM
    blk8 = (ri // 8) == (ci // 8)
    merge_masks = [jnp.logical_and((ri // (2 * m)) == (ci // (2 * m)), (ri // m) != (ci // m))
                   for m in (8, 16, 32)]
    masks = []
    for d in range(2):
        incl = jnp.logical_and(bd, (ri >= ci) if d == 0 else (ri <= ci))
        strict = jnp.logical_and(bd, (ri > ci) if d == 0 else (ri < ci))
        masks.append((incl, strict))

    def split(x):
        return jnp.where(lo, x, 0.0), jnp.where(lo, 0.0, x)

    def chunk_load(c):
        r0 = pl.multiple_of(c * CHUNK, CHUNK)
        return (qs[pl.ds(r0, CHUNK), :] * (HEAD_DIM ** -0.5), ks[pl.ds(r0, CHUNK), :], vs[pl.ds(r0, CHUNK), :],
                bs[pl.ds(r0, CHUNK), :], gs[pl.ds(r0, CHUNK), :], ts[pl.ds(pl.multiple_of(c * 8, 8), 8), :])

    def chunk_par(loaded, d):
        incl, strict = masks[d]
        q, k, v, bt, gt, tl8 = loaded
        ib, ig = 2 * d, 4 + 2 * d
        bcol = jnp.where(lo, bt[:, ib:ib + 1], bt[:, ib + 1:ib + 2])
        gcol = jnp.where(lo, gt[:, ig:ig + 1], gt[:, ig + 1:ig + 2])
        gcol2 = jnp.concatenate([jnp.broadcast_to(gt[:, ig:ig + 1], (CHUNK, LANES)),
                                 jnp.broadcast_to(gt[:, ig + 1:ig + 2], (CHUNK, LANES))], axis=0)
        gtt = jnp.concatenate([gt, gt], axis=0).T
        grow2 = jnp.where(rlo, gtt[ig:ig + 1, :], gtt[ig + 1:ig + 2, :])
        dec = jnp.exp(jnp.where(incl, gcol2 - grow2, NEG_BIG))
        kb = k * bcol
        kb0, kb1 = split(kb)
        q0, q1 = split(q)
        lhs = jnp.concatenate([kb0, kb1, q0, q1], axis=0).astype(BF16)
        kk = jnp.concatenate([k, k], axis=0).astype(BF16)
        x = _nt(lhs, kk)
        vb = v * bcol
        kbg = kb * jnp.exp(gcol)
        vb0, vb1 = split(vb)
        kg0, kg1 = split(kbg)
        rhs = jnp.concatenate([jnp.concatenate([vb0, kg0], axis=1),
                               jnp.concatenate([vb1, kg1], axis=1)], axis=0)
        rhs_b = rhs.astype(BF16)
        tl = jnp.where(lo, tl8[0:1, ig:ig + 1], tl8[0:1, ig + 1:ig + 2])
        ket = (k * jnp.exp(tl - gcol)).T.astype(BF16)
        qg = q * jnp.exp(gcol)
        yield
        low = jnp.where(strict, x[:LANES] * dec, 0.0)
        intra = (x[LANES:] * dec).astype(BF16)
        l0 = jnp.where(blk8, low, 0.0)
        lb = l0.astype(BF16)
        p = _mm(lb, lb)
        cms = [jnp.where(cmask, low, 0.0) for cmask in merge_masks]
        yield
        n = -l0
        y = _mm(jnp.concatenate([n, p], axis=0).astype(BF16), p.astype(BF16))
        yield
        n = n + p + y[:LANES]
        p = y[LANES:]
        y = _mm(n.astype(BF16), p.astype(BF16))
        yield
        n = n + p + y
        for cm in cms:
            y = _mm(n.astype(BF16), cm.astype(BF16))
            yield
            tc = cm + y
            y = _mm(tc.astype(BF16), n.astype(BF16))
            yield
            n = n - tc - y
        y = _mm(n.astype(BF16), rhs_b)
        yield
        uw2 = rhs + y
        iuw2 = _mm(intra, uw2.astype(BF16))
        uw = uw2[:CHUNK] + uw2[CHUNK:]
        u, w = uw[:, :LANES], uw[:, LANES:]
        mb = _mm(ket, jnp.concatenate([w, u], axis=1).astype(BF16))
        yield
        iuw = iuw2[:CHUNK] + iuw2[CHUNK:]
        iu, iw = iuw[:, :LANES], iuw[:, LANES:]
        mraw = jnp.where(bd, mb[:, :LANES], 0.0)
        bn = jnp.where(bd, mb[:, LANES:], 0.0)
        qe = qg - iw
        return qe.astype(BF16), mraw.astype(BF16), bn, iu

    def interleave(gens):
        res = [None] * len(gens)
        live = list(range(len(gens)))
        while live:
            for idx in list(live):
                try:
                    next(gens[idx])
                except StopIteration as stop:
                    res[idx] = stop.value
                    live.remove(idx)
        return res

    def par_body(i, carry):
        cs = [i * PAR_GROUP + j for j in range(PAR_GROUP)]
        loaded = [chunk_load(c) for c in cs]
        flat = interleave([chunk_par(ld, d) for ld in loaded for d in range(2)])
        res = [flat[2 * j:2 * j + 2] for j in range(PAR_GROUP)]
        for c, per_dir in zip(cs, res):
            r0 = pl.multiple_of(c * CHUNK, CHUNK)
            for d, (qe, mraw, bn, iu) in enumerate(per_dir):
                mq[d, c, 0:CHUNK, :] = qe
                mq[d, c, CHUNK:, :] = mraw
                bns[d, c] = bn
                oacc[d, pl.ds(r0, CHUNK), :] = iu
        return carry

    lax.fori_loop(0, n_chunks // PAR_GROUP, par_body, 0)

    def chunk_seq(c, d, s):
        ig = 4 + 2 * d
        r0 = pl.multiple_of(c * CHUNK, CHUNK)
        tl8 = ts[pl.ds(pl.multiple_of(c * 8, 8), 8), :]
        acol = jnp.exp(jnp.where(rlo, tl8[0:1, ig:ig + 1], tl8[0:1, ig + 1:ig + 2]))
        y = _mm(mq[d, c], s.astype(BF16))
        oacc[d, pl.ds(r0, CHUNK), :] = oacc[d, pl.ds(r0, CHUNK), :] + y[:CHUNK]
        return acol * s - y[CHUNK:] + bns[d, c]

    def seq_body(i, carry):
        sf, sb_ = carry
        cb = jnp.where(i < n_ctx_chunks, n_ctx_chunks - 1 - i, n_chunks + n_ctx_chunks - 1 - i)
        return chunk_seq(i, 0, sf), chunk_seq(cb, 1, sb_)

    zero = jnp.zeros((LANES, LANES), F32)
    lax.fori_loop(0, n_chunks, seq_body, (zero, zero))

    ng = ng_ref[...]

    def fin(c, carry):
        r0 = pl.multiple_of(c * CHUNK, CHUNK)
        o = oacc[0, pl.ds(r0, CHUNK), :] + oacc[1, pl.ds(r0, CHUNK), :]
        z = z_ref[0, pl.ds(r0, CHUNK), :]
        y = o * _pair_sumsq_rsqrt(o, lo, RMS_EPS, float(HEAD_DIM)) * ng
        o_ref[0, pl.ds(r0, CHUNK), :] = (y * (z * _sigmoid(z))).astype(o_ref.dtype)
        return carry

    lax.fori_loop(0, n_chunks, fin, 0)


def _gdn(a_qkv, gates, a_z, cw8, alog_l, dtb_l, ng2, n_ctx):
    bsz, tt, _ = a_qkv.shape
    n_pairs = GDN_HEADS // 2
    tok = lambda off: pl.BlockSpec((1, tt, LANES), lambda b, h: (b, 0, h + off))
    par = lambda off: pl.BlockSpec((8, LANES), lambda b, h: (0, h + off))
    vec = pl.BlockSpec((1, LANES), lambda b, h: (0, 0))
    kern = functools.partial(_gdn_kernel, n_ctx_chunks=n_ctx // CHUNK, n_chunks=tt // CHUNK)
    return pl.pallas_call(
        kern,
        grid=(bsz, n_pairs),
        in_specs=[tok(0), tok(n_pairs), tok(2 * n_pairs),
                  pl.BlockSpec((1, tt, LANES), lambda b, h: (b, 0, 0)),
                  tok(0),
                  par(0), par(n_pairs), par(2 * n_pairs), vec, vec, vec],
        out_specs=tok(0),
        out_shape=jax.ShapeDtypeStruct((bsz, tt, A_Z), BF16),
        scratch_shapes=[pltpu.VMEM((tt, LANES), F32)] * 5
                       + [pltpu.VMEM((tt // CHUNK * 8, LANES), F32)]
                       + [pltpu.VMEM((2, tt, LANES), F32),
                          pltpu.VMEM((2, tt // CHUNK, CHUNK + LANES, LANES), BF16),
                          pltpu.VMEM((2, tt // CHUNK, LANES, LANES), F32),
                          pltpu.VMEM((3, CHUNK + 16, LANES), F32)],
        compiler_params=_params(("parallel", "parallel")),
    )(a_qkv, a_qkv, a_qkv, gates, a_z, cw8, cw8, cw8, alog_l, dtb_l, ng2)


def _norm_rope(t, gain, cos, sin, lo, swap_lo):
    y = t * _pair_sumsq_rsqrt(t, lo, RMS_EPS, float(HEAD_DIM)) * gain
    sw = jnp.where(swap_lo, pltpu.roll(y, LANES - 16, 1), pltpu.roll(y, 16, 1))
    return y * cos + sw * sin


def _aprep_kernel(b_ref, c_ref, cos_ref, sin_ref, gn_ref, bq_ref, bk_ref, bv_ref, cq_ref, ck_ref, cv_ref):
    lane = lax.broadcasted_iota(jnp.int32, (1, LANES), 1)
    lo = lane < HEAD_DIM
    swap_lo = (lane % 32) < 16
    cos = cos_ref[...]
    sin = sin_ref[...]
    gn = gn_ref[...]
    scale = HEAD_DIM ** -0.5
    for src, q_ref, k_ref, v_ref, gi in ((b_ref, bq_ref, bk_ref, bv_ref, 0), (c_ref, cq_ref, ck_ref, cv_ref, 2)):
        qg = gn[gi:gi + 1]
        kg = gn[gi + 1:gi + 2]
        for j in range(2):
            t = src[0, :, j * LANES:(j + 1) * LANES]
            q_ref[0, :, j * LANES:(j + 1) * LANES] = (_norm_rope(t, qg, cos, sin, lo, swap_lo) * scale).astype(BF16)
        k_ref[0] = _norm_rope(src[0, :, 2 * LANES:3 * LANES], kg, cos, sin, lo, swap_lo).astype(BF16)
        v_ref[0] = src[0, :, 3 * LANES:4 * LANES].astype(BF16)


def _attn_prep(b_qkv, c_qkv, cos_t, sin_t, gains):
    bsz, tt, _ = b_qkv.shape
    nt = tt // TOK_TILE
    tile = lambda n: pl.BlockSpec((1, TOK_TILE, n), lambda b, t: (b, t, 0))
    tab = pl.BlockSpec((TOK_TILE, LANES), lambda b, t: (t, 0))
    shp = lambda n: jax.ShapeDtypeStruct((bsz, tt, n), BF16)
    return pl.pallas_call(
        _aprep_kernel,
        grid=(bsz, nt),
        in_specs=[tile(B_QKV), tile(C_QKV), tab, tab, pl.BlockSpec((8, LANES), lambda b, t: (0, 0))],
        out_specs=[tile(256), tile(LANES), tile(LANES)] * 2,
        out_shape=[shp(256), shp(LANES), shp(LANES)] * 2,
        compiler_params=_params(("parallel", "parallel")),
    )(b_qkv, c_qkv, cos_t, sin_t, gains)


def _gattn_kernel(q_ref, k_ref, v_ref, o_ref, *, first_tile, n_ctx):
    t = pl.program_id(1) + first_tile
    lane = lax.broadcasted_iota(jnp.int32, (1, LANES), 1)
    lo = lane < HEAD_DIM

    def attend(kk, vv):
        for tile in range(2):
            qt = q_ref[0, :, tile * LANES:(tile + 1) * LANES]
            parts = []
            for hh in range(2):
                qm = jnp.where(lo if hh == 0 else jnp.logical_not(lo), qt, jnp.zeros_like(qt))
                s = _nt(qm, kk)
                mx = jnp.max(s, axis=-1, keepdims=True)
                p = jnp.exp(s - mx)
                l = jnp.sum(p, axis=-1, keepdims=True)
                parts.append(_mm(p.astype(BF16), vv) / l)
            o_ref[0, :, tile * LANES:(tile + 1) * LANES] = jnp.where(lo, parts[0], parts[1]).astype(o_ref.dtype)

    if first_tile == 0:
        @pl.when(t == 0)
        def _():
            attend(k_ref[0, 0:n_ctx, :], v_ref[0, 0:n_ctx, :])

    @pl.when(t > 0)
    def _():
        attend(k_ref[0], v_ref[0])


def _global_attn(q, k, v, n_ctx, need_ctx):
    bsz, tt, _ = q.shape
    first = 0 if need_ctx else n_ctx // TOK_TILE
    nt = tt // TOK_TILE - first
    kv = pl.BlockSpec((1, tt, LANES), lambda b, t: (b, 0, 0))
    return pl.pallas_call(
        functools.partial(_gattn_kernel, first_tile=first, n_ctx=n_ctx),
        grid=(bsz, nt),
        in_specs=[pl.BlockSpec((1, TOK_TILE, 256), lambda b, t: (b, t + first, 0)), kv, kv],
        out_specs=pl.BlockSpec((1, TOK_TILE, 256), lambda b, t: (b, t, 0)),
        out_shape=jax.ShapeDtypeStruct((bsz, nt * TOK_TILE, 256), BF16),
        compiler_params=_params(("parallel", "arbitrary")),
    )(q, k, v)


def _wattn_kernel(q_ref, k_ref, v_ref, sink_ref, o_ref, *, first_tile, n_ctx, n_lat):
    t = pl.program_id(1) + first_tile
    n_ctx_tiles = n_ctx // WQ_TILE
    lane = lax.broadcasted_iota(jnp.int32, (1, LANES), 1)
    lo = lane < HEAD_DIM
    kw = 3 * WQ_TILE
    sink = sink_ref[...]

    def attend(win):
        kc = k_ref[0, 0:n_ctx, :]
        vc = v_ref[0, 0:n_ctx, :]
        if win:
            n = t - n_ctx_tiles
            ws = jnp.clip((n - 1) * WQ_TILE, 0, n_lat - kw)
            start = pl.multiple_of(n_ctx + ws, WQ_TILE)
            kwn = k_ref[0, pl.ds(start, kw), :]
            vwn = v_ref[0, pl.ds(start, kw), :]
            qpos = n * WQ_TILE + lax.broadcasted_iota(jnp.int32, (WQ_TILE, kw), 0)
            kpos = ws + lax.broadcasted_iota(jnp.int32, (WQ_TILE, kw), 1)
            valid = jnp.abs(kpos - qpos) <= WINDOW
        for tile in range(2):
            qt = q_ref[0, :, tile * LANES:(tile + 1) * LANES]
            parts = []
            for hh in range(2):
                head = 2 * hh + tile
                sk = sink[head:head + 1, 0:1]
                qm = jnp.where(lo if hh == 0 else jnp.logical_not(lo), qt, jnp.zeros_like(qt))
                sc = _nt(qm, kc)
                mx = jnp.maximum(jnp.max(sc, axis=-1, keepdims=True), sk)
                if win:
                    sw = jnp.where(valid, _nt(qm, kwn), NEG_BIG)
                    mx = jnp.maximum(mx, jnp.max(sw, axis=-1, keepdims=True))
                pc = jnp.exp(sc - mx)
                l = jnp.sum(pc, axis=-1, keepdims=True) + jnp.exp(sk - mx)
                o = _mm(pc.astype(BF16), vc)
                if win:
                    pw = jnp.exp(sw - mx)
                    l = l + jnp.sum(pw, axis=-1, keepdims=True)
                    o = o + _mm(pw.astype(BF16), vwn)
                parts.append(o / l)
            o_ref[0, :, tile * LANES:(tile + 1) * LANES] = jnp.where(lo, parts[0], parts[1]).astype(o_ref.dtype)

    if first_tile == 0:
        @pl.when(t < n_ctx_tiles)
        def _():
            attend(False)

    @pl.when(t >= n_ctx_tiles)
    def _():
        attend(True)


def _window_attn(q, k, v, sink8, n_ctx, need_ctx):
    bsz, tt, _ = q.shape
    first = 0 if need_ctx else n_ctx // WQ_TILE
    nt = tt // WQ_TILE - first
    kv = pl.BlockSpec((1, tt, LANES), lambda b, t: (b, 0, 0))
    return pl.pallas_call(
        functools.partial(_wattn_kernel, first_tile=first, n_ctx=n_ctx, n_lat=tt - n_ctx),
        grid=(bsz, nt),
        in_specs=[pl.BlockSpec((1, WQ_TILE, 256), lambda b, t: (b, t + first, 0)), kv, kv,
                  pl.BlockSpec((8, LANES), lambda b, t: (0, 0))],
        out_specs=pl.BlockSpec((1, WQ_TILE, 256), lambda b, t: (b, t, 0)),
        out_shape=jax.ShapeDtypeStruct((bsz, nt * WQ_TILE, 256), BF16),
        compiler_params=_params(("parallel", "arbitrary")),
    )(q, k, v, sink8)


def _outmlp_kernel(x_ref, oa_ref, ob_ref, oc_ref, mod_ref, g_ref, wo_ref, w1_ref, w2_ref, o_ref):
    m = mod_ref[0]
    mix = _mm(oa_ref[0], wo_ref[0:A_Z, :])
    mix = mix + _mm(ob_ref[0], wo_ref[A_Z:A_Z + 256, :])
    mix = mix + _mm(oc_ref[0], wo_ref[A_Z + 256:A_Z + 512, :])
    x = x_ref[0] + m[2:3] * mix
    h = _modulated(x, g_ref[...], m[3:4], m[4:5]).astype(BF16)
    fc = 1024
    acc = None
    for f in range(D_FF // fc):
        a = jnp.maximum(_mm(h, w1_ref[:, f * fc:(f + 1) * fc]), 0.0)
        y = _mm((a * a).astype(BF16), w2_ref[f * fc:(f + 1) * fc, :])
        acc = y if acc is None else acc + y
    o_ref[0] = x + m[5:6] * acc


def _out_mlp(xt, oa, ob, oc, mod, g, wo, w1, w2, n_ctx, need_ctx):
    bsz, tt, _ = xt.shape
    first = 0 if need_ctx else n_ctx // TOK_TILE
    nt = tt // TOK_TILE - first
    tile = lambda n: pl.BlockSpec((1, TOK_TILE, n), lambda b, t: (b, t + first, 0))
    att = pl.BlockSpec((1, TOK_TILE, 256), lambda b, t: (b, t, 0))
    full = lambda r, c: pl.BlockSpec((r, c), lambda b, t: (0, 0))
    return pl.pallas_call(
        _outmlp_kernel,
        grid=(bsz, nt),
        in_specs=[tile(D_MODEL), tile(A_Z), att, att,
                  pl.BlockSpec((1, N_MOD, D_MODEL), lambda b, t: (_mod_row(b, t + first), 0, 0)),
                  full(1, D_MODEL), full(D_MODEL, D_MODEL), full(D_MODEL, D_FF), full(D_FF, D_MODEL)],
        out_specs=pl.BlockSpec((1, TOK_TILE, D_MODEL), lambda b, t: (b, t, 0)),
        out_shape=jax.ShapeDtypeStruct((bsz, nt * TOK_TILE, D_MODEL), F32),
        compiler_params=_params(("parallel", "parallel")),
    )(xt, oa, ob, oc, mod, g, wo, w1, w2)


_HEAD_PERM = (0, 2, 1, 3)


def _perm_cols(n_heads_q):
    cols = []
    for h in _HEAD_PERM:
        cols.extend(range(h * HEAD_DIM, (h + 1) * HEAD_DIM))
    return cols


def _gate_cols():
    base_beta = A_QKV + A_Z
    base_alpha = base_beta + A_GATES
    cols = []
    for pair in range(GDN_HEADS // 2):
        for base in (base_beta, base_alpha):
            for d in range(2):
                for j in range(2):
                    cols.append(base + d * GDN_HEADS + 2 * pair + j)
    return cols


def _gate_param_lanes(p):
    out = jnp.zeros((LANES,), F32)
    idx, src = [], []
    for pair in range(GDN_HEADS // 2):
        for d in range(2):
            for j in range(2):
                idx.append(pair * 8 + 4 + d * 2 + j)
                src.append(d * GDN_HEADS + 2 * pair + j)
    return out.at[jnp.array(idx)].set(p.reshape(-1)[jnp.array(src)]).reshape(1, LANES)


def _rope_tables(n_ctx, n_lat):
    rows = n_lat // GRID_W
    row = jnp.repeat(jnp.arange(rows, dtype=F32), GRID_W)
    col = jnp.tile(jnp.arange(GRID_W, dtype=F32), rows)
    half = HEAD_DIM // 4
    inv_freq = ROPE_THETA ** (-jnp.arange(half, dtype=F32) / half)
    ang_r = row[:, None] * inv_freq
    ang_c = col[:, None] * inv_freq
    cr, sr, cc, sc = jnp.cos(ang_r), jnp.sin(ang_r), jnp.cos(ang_c), jnp.sin(ang_c)
    cos = jnp.concatenate([cr, cr, cc, cc], axis=-1)
    sin = jnp.concatenate([-sr, sr, -sc, sc], axis=-1)
    cos = jnp.concatenate([jnp.ones((n_ctx, HEAD_DIM), F32), cos], axis=0)
    sin = jnp.concatenate([jnp.zeros((n_ctx, HEAD_DIM), F32), sin], axis=0)
    return jnp.tile(cos, (1, 2)), jnp.tile(sin, (1, 2))


def _pad_rows(a, rows):
    return jnp.concatenate([a, jnp.zeros((rows - a.shape[0],) + a.shape[1:], a.dtype)], axis=0)


def kernel(x, c, ctx, c_ctx, w_mod, b_mod, g_attn, w_in, gdn_conv_w, gdn_a_log, gdn_dt_bias, gdn_norm_g,
           ga_q_norm_g, ga_k_norm_g, wa_q_norm_g, wa_k_norm_g, wa_sink, w_out, g_mlp, w_mlp_in, w_mlp_out):
    bsz, n_lat, _ = x.shape
    n_ctx = ctx.shape[1]
    depth = w_mod.shape[0]
    assert bsz <= 8 and n_ctx == TOK_TILE and n_lat % TOK_TILE == 0

    xt = jnp.concatenate([ctx, x], axis=1)
    cond16 = _pad_rows(jnp.concatenate([_pad_rows(c, 8), c_ctx[None, :]], axis=0), 16)
    cos_t, sin_t = _rope_tables(n_ctx, n_lat)

    b0 = A_QKV + A_Z + 2 * A_GATES
    c0 = b0 + B_QKV
    qperm = _perm_cols(GA_HEADS)
    in_cols = (list(range(A_QKV + A_Z)) + _gate_cols()
               + [b0 + i for i in qperm] + list(range(b0 + 256, b0 + B_QKV))
               + [c0 + i for i in qperm] + list(range(c0 + 256, c0 + C_QKV)))
    in_cols = np.array(in_cols)
    gate_pad = jnp.zeros((D_MODEL, LANES - 2 * A_GATES), BF16)
    out_rows = np.array(list(range(A_Z)) + [A_Z + i for i in qperm] + [A_Z + 256 + i for i in qperm])

    for l in range(depth):
        need_ctx = l < depth - 1
        mod = _ada_mod(cond16, w_mod[l], b_mod[l]).reshape(16, N_MOD, D_MODEL)
        wl = w_in[l].astype(BF16)[:, in_cols]
        n_front = A_QKV + A_Z + 2 * A_GATES
        w_in_l = jnp.concatenate([wl[:, :n_front], gate_pad, wl[:, n_front:]], axis=1)
        a_qkv, a_z, gates, b_qkv, c_qkv = _in_proj(xt, mod, g_attn[l].reshape(1, D_MODEL), w_in_l)

        o_a = _gdn(a_qkv, gates, a_z, _pad_rows(gdn_conv_w[l], 8), _gate_param_lanes(gdn_a_log[l]),
                   _gate_param_lanes(gdn_dt_bias[l]), jnp.tile(gdn_norm_g[l], 2).reshape(1, LANES), n_ctx)

        gains = _pad_rows(jnp.stack([jnp.tile(g, 2) for g in
                                     (ga_q_norm_g[l], ga_k_norm_g[l], wa_q_norm_g[l], wa_k_norm_g[l])]), 8)
        bq, bk, bv, cq, ck, cv = _attn_prep(b_qkv, c_qkv, cos_t, sin_t, gains)
        o_b = _global_attn(bq, bk, bv, n_ctx, need_ctx)
        sink8 = _pad_rows(jnp.broadcast_to(wa_sink[l][:, None], (WA_HEADS, LANES)), 8)
        o_c = _window_attn(cq, ck, cv, sink8, n_ctx, need_ctx)

        xt = _out_mlp(xt, o_a, o_b, o_c, mod, g_mlp[l].reshape(1, D_MODEL),
                      w_out[l].astype(BF16)[out_rows, :], w_mlp_in[l].astype(BF16),
                      w_mlp_out[l].astype(BF16), n_ctx, need_ctx)
    return xt
```

```python
import functools
import math

import numpy as np
import jax
import jax.numpy as jnp
from jax import lax
from jax.experimental import pallas as pl
from jax.experimental.pallas import tpu as pltpu

F32 = jnp.float32
BF16 = jnp.bfloat16

D_MODEL = 1024
GRID_W = 64
HEAD_DIM = 64
ROPE_THETA = 10000.0
GDN_HEADS = 8
GDN_CONV = 5
CHUNK = 64
GA_HEADS = 4
WA_HEADS = 4
WINDOW = 128
D_FF = 4 * D_MODEL
N_MOD = 6
RMS_EPS = 1e-6
A_QKV = 1536
A_Z = 512
A_GATES = 16
B_QKV = 512
C_QKV = 512
LANES = 128
TOK_TILE = 256
WQ_TILE = 128
PREP_UNROLL = 2
FIN_UNROLL = 4
PAR_GROUP = 9
NEG_BIG = -1e30
VMEM_LIMIT = 56 * 1024 * 1024


def _mm(a, b):
    return jnp.dot(a, b, preferred_element_type=F32)


def _nt(a, b):
    return lax.dot_general(a, b, (((1,), (1,)), ((), ())), preferred_element_type=F32)


def _sigmoid(x):
    return 1.0 / (1.0 + jnp.exp(-x))


def _params(sem):
    return pltpu.CompilerParams(dimension_semantics=sem, vmem_limit_bytes=VMEM_LIMIT)


def _mod_kernel(c_ref, w_ref, b_ref, o_ref):
    c = c_ref[...]
    s = c * _sigmoid(c)
    o_ref[...] = _mm(s.astype(BF16), w_ref[...].astype(BF16)) + b_ref[...]


def _ada_mod(cond16, w_mod, b_mod):
    n = N_MOD * D_MODEL
    tn = 1024
    return pl.pallas_call(
        _mod_kernel,
        grid=(n // tn,),
        in_specs=[pl.BlockSpec((16, D_MODEL), lambda j: (0, 0)),
                  pl.BlockSpec((D_MODEL, tn), lambda j: (0, j)),
                  pl.BlockSpec((1, tn), lambda j: (0, j))],
        out_specs=pl.BlockSpec((16, tn), lambda j: (0, j)),
        out_shape=jax.ShapeDtypeStruct((16, n), F32),
        compiler_params=_params(("arbitrary",)),
    )(cond16, w_mod, b_mod.reshape(1, n))


def _mod_row(b, t):
    return jnp.where(t == 0, 8, b)


def _modulated(x, g, shift, scale):
    ms = jnp.mean(x * x, axis=-1, keepdims=True)
    y = x * lax.rsqrt(ms + RMS_EPS) * g
    return y * (1.0 + scale) + shift


IN_SPLITS = (A_QKV, A_Z, LANES, B_QKV, C_QKV)
IN_COLS = sum(IN_SPLITS)


def _pair_sumsq_rsqrt(y, lo, eps, denom):
    yy = y * y
    s0 = jnp.sum(jnp.where(lo, yy, 0.0), axis=-1, keepdims=True)
    s1 = jnp.sum(jnp.where(lo, 0.0, yy), axis=-1, keepdims=True)
    return jnp.where(lo, lax.rsqrt(s0 * (1.0 / denom) + eps), lax.rsqrt(s1 * (1.0 / denom) + eps))


def _norm_rope(t, gain, cos, sin, lo, swap_lo):
    y = t * _pair_sumsq_rsqrt(t, lo, RMS_EPS, float(HEAD_DIM)) * gain
    sw = jnp.where(swap_lo, pltpu.roll(y, LANES - 16, 1), pltpu.roll(y, 16, 1))
    return y * cos + sw * sin


def _inproj_kernel(x_ref, mod_ref, g_ref, w_ref, cos_ref, sin_ref, gn_ref,
                   oa_ref, oz_ref, og_ref, bq_ref, bk_ref, bv_ref, cq_ref, ck_ref, cv_ref):
    m = mod_ref[0]
    h = _modulated(x_ref[0], g_ref[...], m[0:1], m[1:2]).astype(BF16)
    off = 0
    for o_ref, n in ((oa_ref, A_QKV), (oz_ref, A_Z), (og_ref, LANES)):
        o_ref[0] = _mm(h, w_ref[:, off:off + n])
        off += n
    lane = lax.broadcasted_iota(jnp.int32, (1, LANES), 1)
    lo = lane < HEAD_DIM
    swap_lo = (lane % 32) < 16
    cos = cos_ref[...]
    sin = sin_ref[...]
    gn = gn_ref[...]
    scale = HEAD_DIM ** -0.5
    for q_ref, k_ref, v_ref, gi in ((bq_ref, bk_ref, bv_ref, 0), (cq_ref, ck_ref, cv_ref, 2)):
        p = _mm(h, w_ref[:, off:off + B_QKV])
        off += B_QKV
        for j in range(2):
            t = p[:, j * LANES:(j + 1) * LANES]
            q_ref[0, :, j * LANES:(j + 1) * LANES] = (
                _norm_rope(t, gn[gi:gi + 1], cos, sin, lo, swap_lo) * scale).astype(BF16)
        k_ref[0] = _norm_rope(p[:, 2 * LANES:3 * LANES], gn[gi + 1:gi + 2], cos, sin, lo, swap_lo).astype(BF16)
        v_ref[0] = p[:, 3 * LANES:].astype(BF16)


def _in_proj(xt, mod, g, w, cos_t, sin_t, gains):
    bsz, tt, _ = xt.shape
    nt = tt // TOK_TILE
    tile = lambda n: pl.BlockSpec((1, TOK_TILE, n), lambda b, t: (b, t, 0))
    tab = pl.BlockSpec((TOK_TILE, LANES), lambda b, t: (t, 0))
    f32 = lambda n: jax.ShapeDtypeStruct((bsz, tt, n), F32)
    b16 = lambda n: jax.ShapeDtypeStruct((bsz, tt, n), BF16)
    return pl.pallas_call(
        _inproj_kernel,
        grid=(bsz, nt),
        in_specs=[tile(D_MODEL),
                  pl.BlockSpec((1, N_MOD, D_MODEL), lambda b, t: (_mod_row(b, t), 0, 0)),
                  pl.BlockSpec((1, D_MODEL), lambda b, t: (0, 0)),
                  pl.BlockSpec((D_MODEL, IN_COLS), lambda b, t: (0, 0)),
                  tab, tab, pl.BlockSpec((8, LANES), lambda b, t: (0, 0))],
        out_specs=[tile(A_QKV), tile(A_Z), tile(LANES)] + [tile(256), tile(LANES), tile(LANES)] * 2,
        out_shape=[f32(A_QKV), f32(A_Z), f32(LANES)] + [b16(256), b16(LANES), b16(LANES)] * 2,
        compiler_params=_params(("parallel", "parallel")),
    )(xt, mod, g, w, cos_t, sin_t, gains)


def _gdn_kernel(q_ref, k_ref, v_ref, gate_ref, z_ref, cwq_ref, cwk_ref, cwv_ref, alog_ref, dtb_ref,
                ng_ref, o_ref, qs, ks, vs, bs, gs, ts, oacc, mq, bns, ext, *, n_ctx_chunks, n_chunks):
    hp = pl.program_id(1)
    tt = n_chunks * CHUNK
    lane = lax.broadcasted_iota(jnp.int32, (1, LANES), 1)
    lo = lane < HEAD_DIM
    bwd_lane = (lane % 4) >= 2
    r64 = lax.broadcasted_iota(jnp.int32, (CHUNK, CHUNK), 0)
    c64 = lax.broadcasted_iota(jnp.int32, (CHUNK, CHUNK), 1)
    tri = jnp.where(r64 >= c64, 1.0, 0.0).astype(BF16)
    shift = (LANES - 8 * hp) % LANES
    alog = alog_ref[...]
    dtb = dtb_ref[...]

    def prep(c, slot):
        r0 = pl.multiple_of(c * CHUNK, CHUNK)
        first = jnp.logical_or(c == 0, c == n_ctx_chunks)
        last = jnp.logical_or(c == n_ctx_chunks - 1, c == n_chunks - 1)
        for i, (src, cw_ref, dst, norm) in enumerate(((q_ref, cwq_ref, qs, True),
                                                      (k_ref, cwk_ref, ks, True),
                                                      (v_ref, cwv_ref, vs, False))):
            prev = src[0, pl.ds(jnp.maximum(r0 - 8, 0), 8), :]
            nxt = src[0, pl.ds(jnp.minimum(r0 + CHUNK, tt - 8), 8), :]
            ext[slot + i,0:8, :] = jnp.where(first, 0.0, prev)
            ext[slot + i,8:8 + CHUNK, :] = src[0, pl.ds(r0, CHUNK), :]
            ext[slot + i,8 + CHUNK:16 + CHUNK, :] = jnp.where(last, 0.0, nxt)
            cw = cw_ref[...]
            y = ext[slot + i,6:6 + CHUNK, :] * cw[0:1]
            for j in range(1, GDN_CONV):
                y = y + ext[slot + i,6 + j:6 + j + CHUNK, :] * cw[j:j + 1]
            y = y * _sigmoid(y)
            if norm:
                y = y * _pair_sumsq_rsqrt(y, lo, 1e-6, 1.0)
            dst[pl.ds(r0, CHUNK), :] = y
        gt = gate_ref[0, pl.ds(r0, CHUNK), :]
        beta = _sigmoid(gt)
        a = gt + dtb
        sp = jnp.maximum(a, 0.0) + jnp.log(1.0 + jnp.exp(-jnp.abs(a)))
        g = -jnp.exp(alog) * sp
        g_hi = g.astype(BF16)
        r1 = g - g_hi.astype(F32)
        g_mid = r1.astype(BF16)
        g_lo = (r1 - g_mid.astype(F32)).astype(BF16)
        gpre = _mm(tri, g_hi) + _mm(tri, g_mid) + _mm(tri, g_lo)
        tot = gpre[CHUNK - 1:CHUNK, :]
        gsuf = tot - gpre + g
        gc = jnp.where(bwd_lane, gsuf, gpre)
        bs[pl.ds(r0, CHUNK), :] = pltpu.roll(beta, shift, 1)
        gs[pl.ds(r0, CHUNK), :] = pltpu.roll(gc, shift, 1)
        ts[pl.ds(pl.multiple_of(c * 8, 8), 8), :] = pltpu.roll(jnp.broadcast_to(tot, (8, LANES)), shift, 1)

    def prep_body(i, carry):
        for u in range(PREP_UNROLL):
            prep(i * PREP_UNROLL + u, 3 * u)
        return carry

    lax.fori_loop(0, n_chunks // PREP_UNROLL, prep_body, 0)

    ri = lax.broadcasted_iota(jnp.int32, (LANES, LANES), 0)
    ci = lax.broadcasted_iota(jnp.int32, (LANES, LANES), 1)
    bd = (ri < HEAD_DIM) == (ci < HEAD_DIM)
    ti = lax.broadcasted_iota(jnp.int32, (CHUNK, LANES), 0)
    tj = lax.broadcasted_iota(jnp.int32, (CHUNK, LANES), 1) % CHUNK
    blk8 = (ti // 8) == (tj // 8)
    merge_masks = [jnp.logical_and((ti // (2 * m)) == (tj // (2 * m)), (ti // m) != (tj // m))
                   for m in (8, 16, 32)]
    masks = [((ti >= tj), (ti > tj)), ((ti <= tj), (ti < tj))]

    def blockdiag(x):
        xb = x.astype(BF16)
        return jnp.where(bd, jnp.concatenate([xb, xb], axis=0), jnp.zeros((LANES, LANES), BF16))

    def chunk_par(c, d):
        incl, strict = masks[d]
        r0 = pl.multiple_of(c * CHUNK, CHUNK)
        ib, ig = 2 * d, 4 + 2 * d

        def gate_cols():
            bt = bs[pl.ds(r0, CHUNK), :]
            gt = gs[pl.ds(r0, CHUNK), :]
            return (jnp.where(lo, bt[:, ib:ib + 1], bt[:, ib + 1:ib + 2]),
                    jnp.where(lo, gt[:, ig:ig + 1], gt[:, ig + 1:ig + 2]))

        bcol, _ = gate_cols()
        k = ks[pl.ds(r0, CHUNK), :]
        q = qs[pl.ds(r0, CHUNK), :] * (HEAD_DIM ** -0.5)
        x = _nt(jnp.concatenate([k * bcol, q], axis=0).astype(BF16), blockdiag(k))
        yield
        gt = gs[pl.ds(r0, CHUNK), :]
        gcol = jnp.where(lo, gt[:, ig:ig + 1], gt[:, ig + 1:ig + 2])
        gtt = jnp.concatenate([gt, gt], axis=0).T
        grow = jnp.where(lo, gtt[ig:ig + 1, :], gtt[ig + 1:ig + 2, :])
        dec = jnp.exp(jnp.where(incl, gcol - grow, NEG_BIG))
        low = jnp.where(strict, x[:CHUNK] * dec, 0.0)
        intra = (x[CHUNK:] * dec).astype(BF16)
        l0 = jnp.where(blk8, low, 0.0)
        p = _mm(l0.astype(BF16), blockdiag(l0))
        yield
        n = -l0
        y = _mm(jnp.concatenate([n, p], axis=0).astype(BF16), blockdiag(p))
        yield
        n = n + p + y[:CHUNK]
        p = y[CHUNK:]
        y = _mm(n.astype(BF16), blockdiag(p))
        yield
        n = n + p + y
        for cmask in merge_masks:
            cm = jnp.where(cmask, low, 0.0)
            y = _mm(n.astype(BF16), blockdiag(cm))
            yield
            tc = cm + y
            y = _mm(tc.astype(BF16), blockdiag(n))
            yield
            n = n - tc - y
        bcol, gcol = gate_cols()
        k = ks[pl.ds(r0, CHUNK), :]
        vb = vs[pl.ds(r0, CHUNK), :] * bcol
        kbg = k * bcol * jnp.exp(gcol)
        y = _mm(n.astype(BF16), jnp.concatenate([blockdiag(vb), blockdiag(kbg)], axis=1))
        yield
        u = vb + y[:, :LANES]
        w = kbg + y[:, LANES:]
        bu, bw = blockdiag(u), blockdiag(w)
        iuw = _mm(intra, jnp.concatenate([bu, bw], axis=1))
        tl8 = ts[pl.ds(pl.multiple_of(c * 8, 8), 8), :]
        tl = jnp.where(lo, tl8[0:1, ig:ig + 1], tl8[0:1, ig + 1:ig + 2])
        ke = k * jnp.exp(tl - gcol)
        kett = jnp.concatenate([ke, ke], axis=0).T
        ket = jnp.where(lo, kett[:CHUNK], kett[CHUNK:]).astype(BF16)
        mb = _mm(ket, jnp.concatenate([bw, bu], axis=1))
        yield
        _, gcol = gate_cols()
        qe = qs[pl.ds(r0, CHUNK), :] * (HEAD_DIM ** -0.5) * jnp.exp(gcol) - iuw[:, LANES:]
        return jnp.concatenate([qe, mb[:, :LANES]], axis=0).astype(BF16), mb[:, LANES:], iuw[:, :LANES]

    def interleave(gens):
        res = [None] * len(gens)
        live = list(range(len(gens)))
        while live:
            for idx in list(live):
                try:
                    next(gens[idx])
                except StopIteration as stop:
                    res[idx] = stop.value
                    live.remove(idx)
        return res

    def par_body(i, carry):
        cs = [i * PAR_GROUP + j for j in range(PAR_GROUP)]
        flat = interleave([chunk_par(c, d) for c in cs for d in range(2)])
        res = [flat[2 * j:2 * j + 2] for j in range(PAR_GROUP)]
        for c, per_dir in zip(cs, res):
            r0 = pl.multiple_of(c * CHUNK, CHUNK)
            for d, (qm, bn, iu) in enumerate(per_dir):
                mq[d, c] = qm
                bns[d, c] = bn
                oacc[d, pl.ds(r0, CHUNK), :] = iu
        return carry

    lax.fori_loop(0, n_chunks // PAR_GROUP, par_body, 0)

    def chunk_seq(c, d, s):
        ig = 4 + 2 * d
        r0 = pl.multiple_of(c * CHUNK, CHUNK)
        tl8 = ts[pl.ds(pl.multiple_of(c * 8, 8), 8), :]
        arow = jnp.exp(jnp.where(lo, tl8[0:1, ig:ig + 1], tl8[0:1, ig + 1:ig + 2]))
        y = _mm(mq[d, c], blockdiag(s))
        oacc[d, pl.ds(r0, CHUNK), :] = oacc[d, pl.ds(r0, CHUNK), :] + y[:CHUNK]
        return arow * s - y[CHUNK:] + bns[d, c]

    def seq_body(i, carry):
        sf, sb_ = carry
        cb = jnp.where(i < n_ctx_chunks, n_ctx_chunks - 1 - i, n_chunks + n_ctx_chunks - 1 - i)
        return chunk_seq(i, 0, sf), chunk_seq(cb, 1, sb_)

    zero = jnp.zeros((CHUNK, LANES), F32)
    lax.fori_loop(0, n_chunks, seq_body, (zero, zero))

    ng = ng_ref[...]

    def fin(i, carry):
        for u in range(FIN_UNROLL):
            r0 = pl.multiple_of((i * FIN_UNROLL + u) * CHUNK, CHUNK)
            o = oacc[0, pl.ds(r0, CHUNK), :] + oacc[1, pl.ds(r0, CHUNK), :]
            z = z_ref[0, pl.ds(r0, CHUNK), :]
            y = o * _pair_sumsq_rsqrt(o, lo, RMS_EPS, float(HEAD_DIM)) * ng
            o_ref[0, pl.ds(r0, CHUNK), :] = (y * (z * _sigmoid(z))).astype(o_ref.dtype)
        return carry

    lax.fori_loop(0, n_chunks // FIN_UNROLL, fin, 0)


def _gdn(a_qkv, gates, a_z, cw8, alog_l, dtb_l, ng2, n_ctx):
    bsz, tt, _ = a_qkv.shape
    n_pairs = GDN_HEADS // 2
    tok = lambda off: pl.BlockSpec((1, tt, LANES), lambda b, h: (b, 0, h + off))
    par = lambda off: pl.BlockSpec((8, LANES), lambda b, h: (0, h + off))
    vec = pl.BlockSpec((1, LANES), lambda b, h: (0, 0))
    kern = functools.partial(_gdn_kernel, n_ctx_chunks=n_ctx // CHUNK, n_chunks=tt // CHUNK)
    return pl.pallas_call(
        kern,
        grid=(bsz, n_pairs),
        in_specs=[tok(0), tok(n_pairs), tok(2 * n_pairs),
                  pl.BlockSpec((1, tt, LANES), lambda b, h: (b, 0, 0)),
                  tok(0),
                  par(0), par(n_pairs), par(2 * n_pairs), vec, vec, vec],
        out_specs=tok(0),
        out_shape=jax.ShapeDtypeStruct((bsz, tt, A_Z), BF16),
        scratch_shapes=[pltpu.VMEM((tt, LANES), F32)] * 5
                       + [pltpu.VMEM((tt // CHUNK * 8, LANES), F32)]
                       + [pltpu.VMEM((2, tt, LANES), F32),
                          pltpu.VMEM((2, tt // CHUNK, 2 * CHUNK, LANES), BF16),
                          pltpu.VMEM((2, tt // CHUNK, CHUNK, LANES), F32),
                          pltpu.VMEM((3 * PREP_UNROLL, CHUNK + 16, LANES), F32)],
        compiler_params=_params(("parallel", "parallel")),
    )(a_qkv, a_qkv, a_qkv, gates, a_z, cw8, cw8, cw8, alog_l, dtb_l, ng2)


def _head_queries(q_ref, lo):
    qms = []
    for tile in range(2):
        qt = q_ref[0, :, tile * LANES:(tile + 1) * LANES]
        for hh in range(2):
            qms.append(jnp.where(lo if hh == 0 else jnp.logical_not(lo), qt, jnp.zeros_like(qt)))
    return qms


def _store_heads(o_ref, outs, lo):
    for tile in range(2):
        o_ref[0, :, tile * LANES:(tile + 1) * LANES] = jnp.where(
            lo, outs[2 * tile], outs[2 * tile + 1]).astype(o_ref.dtype)


def _gattn_kernel(q_ref, k_ref, v_ref, o_ref, *, first_tile, n_ctx):
    t = pl.program_id(1) + first_tile
    lane = lax.broadcasted_iota(jnp.int32, (1, LANES), 1)
    lo = lane < HEAD_DIM

    def attend(kk, vv):
        scores = [_nt(qm, kk) for qm in _head_queries(q_ref, lo)]
        probs, sums = [], []
        for s in scores:
            p = jnp.exp(s - jnp.max(s, axis=-1, keepdims=True))
            sums.append(jnp.sum(p, axis=-1, keepdims=True))
            probs.append(p.astype(BF16))
        _store_heads(o_ref, [_mm(p, vv) / l for p, l in zip(probs, sums)], lo)

    if first_tile == 0:
        @pl.when(t == 0)
        def _():
            attend(k_ref[0, 0:n_ctx, :], v_ref[0, 0:n_ctx, :])

    @pl.when(t > 0)
    def _():
        attend(k_ref[0], v_ref[0])


def _global_attn(q, k, v, n_ctx, need_ctx):
    bsz, tt, _ = q.shape
    first = 0 if need_ctx else n_ctx // TOK_TILE
    nt = tt // TOK_TILE - first
    kv = pl.BlockSpec((1, tt, LANES), lambda b, t: (b, 0, 0))
    return pl.pallas_call(
        functools.partial(_gattn_kernel, first_tile=first, n_ctx=n_ctx),
        grid=(bsz, nt),
        in_specs=[pl.BlockSpec((1, TOK_TILE, 256), lambda b, t: (b, t + first, 0)), kv, kv],
        out_specs=pl.BlockSpec((1, TOK_TILE, 256), lambda b, t: (b, t, 0)),
        out_shape=jax.ShapeDtypeStruct((bsz, nt * TOK_TILE, 256), BF16),
        compiler_params=_params(("parallel", "arbitrary")),
    )(q, k, v)


def _wattn_kernel(q_ref, k_ref, v_ref, sink_ref, o_ref, *, first_tile, n_ctx, n_lat):
    t = pl.program_id(1) + first_tile
    n_ctx_tiles = n_ctx // WQ_TILE
    lane = lax.broadcasted_iota(jnp.int32, (1, LANES), 1)
    lo = lane < HEAD_DIM
    kw = 3 * WQ_TILE
    sink = sink_ref[...]
    sinks = [sink[2 * hh + tile:2 * hh + tile + 1, 0:1] for tile in range(2) for hh in range(2)]

    def attend(win):
        kc = k_ref[0, 0:n_ctx, :]
        vc = v_ref[0, 0:n_ctx, :]
        qms = _head_queries(q_ref, lo)
        s_ctx = [_nt(qm, kc) for qm in qms]
        mxs = [jnp.maximum(jnp.max(s, axis=-1, keepdims=True), sk) for s, sk in zip(s_ctx, sinks)]
        if win:
            n = t - n_ctx_tiles
            ws = jnp.clip((n - 1) * WQ_TILE, 0, n_lat - kw)
            start = pl.multiple_of(n_ctx + ws, WQ_TILE)
            kwn = k_ref[0, pl.ds(start, kw), :]
            vwn = v_ref[0, pl.ds(start, kw), :]
            qpos = n * WQ_TILE + lax.broadcasted_iota(jnp.int32, (WQ_TILE, kw), 0)
            kpos = ws + lax.broadcasted_iota(jnp.int32, (WQ_TILE, kw), 1)
            valid = jnp.abs(kpos - qpos) <= WINDOW
            s_win = [jnp.where(valid, _nt(qm, kwn), NEG_BIG) for qm in qms]
            mxs = [jnp.maximum(mx, jnp.max(s, axis=-1, keepdims=True)) for mx, s in zip(mxs, s_win)]
        p_ctx = [jnp.exp(s - mx) for s, mx in zip(s_ctx, mxs)]
        sums = [jnp.sum(p, axis=-1, keepdims=True) + jnp.exp(sk - mx) for p, sk, mx in zip(p_ctx, sinks, mxs)]
        outs = [_mm(p.astype(BF16), vc) for p in p_ctx]
        if win:
            p_win = [jnp.exp(s - mx) for s, mx in zip(s_win, mxs)]
            sums = [l + jnp.sum(p, axis=-1, keepdims=True) for l, p in zip(sums, p_win)]
            outs = [o + _mm(p.astype(BF16), vwn) for o, p in zip(outs, p_win)]
        _store_heads(o_ref, [o / l for o, l in zip(outs, sums)], lo)

    if first_tile == 0:
        @pl.when(t < n_ctx_tiles)
        def _():
            attend(False)

    @pl.when(t >= n_ctx_tiles)
    def _():
        attend(True)


def _window_attn(q, k, v, sink8, n_ctx, need_ctx):
    bsz, tt, _ = q.shape
    first = 0 if need_ctx else n_ctx // WQ_TILE
    nt = tt // WQ_TILE - first
    kv = pl.BlockSpec((1, tt, LANES), lambda b, t: (b, 0, 0))
    return pl.pallas_call(
        functools.partial(_wattn_kernel, first_tile=first, n_ctx=n_ctx, n_lat=tt - n_ctx),
        grid=(bsz, nt),
        in_specs=[pl.BlockSpec((1, WQ_TILE, 256), lambda b, t: (b, t + first, 0)), kv, kv,
                  pl.BlockSpec((8, LANES), lambda b, t: (0, 0))],
        out_specs=pl.BlockSpec((1, WQ_TILE, 256), lambda b, t: (b, t, 0)),
        out_shape=jax.ShapeDtypeStruct((bsz, nt * WQ_TILE, 256), BF16),
        compiler_params=_params(("parallel", "arbitrary")),
    )(q, k, v, sink8)


def _outmlp_kernel(x_ref, oa_ref, ob_ref, oc_ref, mod_ref, g_ref, wo_ref, w1_ref, w2_ref, o_ref):
    m = mod_ref[0]
    mix = _mm(oa_ref[0], wo_ref[0:A_Z, :])
    mix = mix + _mm(ob_ref[0], wo_ref[A_Z:A_Z + 256, :])
    mix = mix + _mm(oc_ref[0], wo_ref[A_Z + 256:A_Z + 512, :])
    x = x_ref[0] + m[2:3] * mix
    h = _modulated(x, g_ref[...], m[3:4], m[4:5]).astype(BF16)
    fc = 1024
    acc = None
    for f in range(D_FF // fc):
        a = jnp.maximum(_mm(h, w1_ref[:, f * fc:(f + 1) * fc]), 0.0)
        y = _mm((a * a).astype(BF16), w2_ref[f * fc:(f + 1) * fc, :])
        acc = y if acc is None else acc + y
    o_ref[0] = x + m[5:6] * acc


def _out_mlp(xt, oa, ob, oc, mod, g, wo, w1, w2, n_ctx, need_ctx):
    bsz, tt, _ = xt.shape
    first = 0 if need_ctx else n_ctx // TOK_TILE
    nt = tt // TOK_TILE - first
    tile = lambda n: pl.BlockSpec((1, TOK_TILE, n), lambda b, t: (b, t + first, 0))
    att = pl.BlockSpec((1, TOK_TILE, 256), lambda b, t: (b, t, 0))
    full = lambda r, c: pl.BlockSpec((r, c), lambda b, t: (0, 0))
    return pl.pallas_call(
        _outmlp_kernel,
        grid=(bsz, nt),
        in_specs=[tile(D_MODEL), tile(A_Z), att, att,
                  pl.BlockSpec((1, N_MOD, D_MODEL), lambda b, t: (_mod_row(b, t + first), 0, 0)),
                  full(1, D_MODEL), full(D_MODEL, D_MODEL), full(D_MODEL, D_FF), full(D_FF, D_MODEL)],
        out_specs=pl.BlockSpec((1, TOK_TILE, D_MODEL), lambda b, t: (b, t, 0)),
        out_shape=jax.ShapeDtypeStruct((bsz, nt * TOK_TILE, D_MODEL), F32),
        compiler_params=_params(("parallel", "parallel")),
    )(xt, oa, ob, oc, mod, g, wo, w1, w2)


_HEAD_PERM = (0, 2, 1, 3)


def _perm_cols(n_heads_q):
    cols = []
    for h in _HEAD_PERM:
        cols.extend(range(h * HEAD_DIM, (h + 1) * HEAD_DIM))
    return cols


def _gate_cols():
    base_beta = A_QKV + A_Z
    base_alpha = base_beta + A_GATES
    cols = []
    for pair in range(GDN_HEADS // 2):
        for base in (base_beta, base_alpha):
            for d in range(2):
                for j in range(2):
                    cols.append(base + d * GDN_HEADS + 2 * pair + j)
    return cols


def _gate_param_lanes(p):
    out = jnp.zeros((LANES,), F32)
    idx, src = [], []
    for pair in range(GDN_HEADS // 2):
        for d in range(2):
            for j in range(2):
                idx.append(pair * 8 + 4 + d * 2 + j)
                src.append(d * GDN_HEADS + 2 * pair + j)
    return out.at[jnp.array(idx)].set(p.reshape(-1)[jnp.array(src)]).reshape(1, LANES)


def _rope_tables(n_ctx, n_lat):
    rows = n_lat // GRID_W
    row = jnp.repeat(jnp.arange(rows, dtype=F32), GRID_W)
    col = jnp.tile(jnp.arange(GRID_W, dtype=F32), rows)
    half = HEAD_DIM // 4
    inv_freq = ROPE_THETA ** (-jnp.arange(half, dtype=F32) / half)
    ang_r = row[:, None] * inv_freq
    ang_c = col[:, None] * inv_freq
    cr, sr, cc, sc = jnp.cos(ang_r), jnp.sin(ang_r), jnp.cos(ang_c), jnp.sin(ang_c)
    cos = jnp.concatenate([cr, cr, cc, cc], axis=-1)
    sin = jnp.concatenate([-sr, sr, -sc, sc], axis=-1)
    cos = jnp.concatenate([jnp.ones((n_ctx, HEAD_DIM), F32), cos], axis=0)
    sin = jnp.concatenate([jnp.zeros((n_ctx, HEAD_DIM), F32), sin], axis=0)
    return jnp.tile(cos, (1, 2)), jnp.tile(sin, (1, 2))


def _pad_rows(a, rows):
    return jnp.concatenate([a, jnp.zeros((rows - a.shape[0],) + a.shape[1:], a.dtype)], axis=0)


def kernel(x, c, ctx, c_ctx, w_mod, b_mod, g_attn, w_in, gdn_conv_w, gdn_a_log, gdn_dt_bias, gdn_norm_g,
           ga_q_norm_g, ga_k_norm_g, wa_q_norm_g, wa_k_norm_g, wa_sink, w_out, g_mlp, w_mlp_in, w_mlp_out):
    bsz, n_lat, _ = x.shape
    n_ctx = ctx.shape[1]
    depth = w_mod.shape[0]
    assert bsz <= 8 and n_ctx == TOK_TILE and n_lat % TOK_TILE == 0

    xt = jnp.concatenate([ctx, x], axis=1)
    cond16 = _pad_rows(jnp.concatenate([_pad_rows(c, 8), c_ctx[None, :]], axis=0), 16)
    cos_t, sin_t = _rope_tables(n_ctx, n_lat)

    b0 = A_QKV + A_Z + 2 * A_GATES
    c0 = b0 + B_QKV
    qperm = _perm_cols(GA_HEADS)
    in_cols = (list(range(A_QKV + A_Z)) + _gate_cols()
               + [b0 + i for i in qperm] + list(range(b0 + 256, b0 + B_QKV))
               + [c0 + i for i in qperm] + list(range(c0 + 256, c0 + C_QKV)))
    in_cols = np.array(in_cols)
    gate_pad = jnp.zeros((D_MODEL, LANES - 2 * A_GATES), BF16)
    out_rows = np.array(list(range(A_Z)) + [A_Z + i for i in qperm] + [A_Z + 256 + i for i in qperm])

    for l in range(depth):
        need_ctx = l < depth - 1
        mod = _ada_mod(cond16, w_mod[l], b_mod[l]).reshape(16, N_MOD, D_MODEL)
        wl = w_in[l].astype(BF16)[:, in_cols]
        n_front = A_QKV + A_Z + 2 * A_GATES
        w_in_l = jnp.concatenate([wl[:, :n_front], gate_pad, wl[:, n_front:]], axis=1)
        gains = _pad_rows(jnp.stack([jnp.tile(g, 2) for g in
                                     (ga_q_norm_g[l], ga_k_norm_g[l], wa_q_norm_g[l], wa_k_norm_g[l])]), 8)
        a_qkv, a_z, gates, bq, bk, bv, cq, ck, cv = _in_proj(xt, mod, g_attn[l].reshape(1, D_MODEL), w_in_l,
                                                            cos_t, sin_t, gains)

        o_a = _gdn(a_qkv, gates, a_z, _pad_rows(gdn_conv_w[l], 8), _gate_param_lanes(gdn_a_log[l]),
                   _gate_param_lanes(gdn_dt_bias[l]), jnp.tile(gdn_norm_g[l], 2).reshape(1, LANES), n_ctx)

        o_b = _global_attn(bq, bk, bv, n_ctx, need_ctx)
        sink8 = _pad_rows(jnp.broadcast_to(wa_sink[l][:, None], (WA_HEADS, LANES)), 8)
        o_c = _window_attn(cq, ck, cv, sink8, n_ctx, need_ctx)

        xt = _out_mlp(xt, o_a, o_b, o_c, mod, g_mlp[l].reshape(1, D_MODEL),
                      w_out[l].astype(BF16)[out_rows, :], w_mlp_in[l].astype(BF16),
                      w_mlp_out[l].astype(BF16), n_ctx, need_ctx)
    return xt
```

```python
import functools
import math

import numpy as np
import jax
import jax.numpy as jnp
from jax import lax
from jax.experimental import pallas as pl
from jax.experimental.pallas import tpu as pltpu

F32 = jnp.float32
BF16 = jnp.bfloat16

D_MODEL = 1024
GRID_W = 64
HEAD_DIM = 64
ROPE_THETA = 10000.0
GDN_HEADS = 8
GDN_CONV = 5
CHUNK = 64
GA_HEADS = 4
WA_HEADS = 4
WINDOW = 128
D_FF = 4 * D_MODEL
N_MOD = 6
RMS_EPS = 1e-6
A_QKV = 1536
A_Z = 512
A_GATES = 16
B_QKV = 512
C_QKV = 512
LANES = 128
TOK_TILE = 256
WQ_TILE = 256
PREP_UNROLL = 4
FIN_UNROLL = 4
PAR_GROUP = 9
NEG_BIG = -1e30
VMEM_LIMIT = 56 * 1024 * 1024


def _mm(a, b):
    return jnp.dot(a, b, preferred_element_type=F32)


def _nt(a, b):
    return lax.dot_general(a, b, (((1,), (1,)), ((), ())), preferred_element_type=F32)


def _sigmoid(x):
    return 1.0 / (1.0 + jnp.exp(-x))


def _params(sem):
    return pltpu.CompilerParams(dimension_semantics=sem, vmem_limit_bytes=VMEM_LIMIT)


def _mod_kernel(c_ref, w_ref, b_ref, o_ref):
    c = c_ref[...]
    s = c * _sigmoid(c)
    o_ref[...] = _mm(s.astype(BF16), w_ref[...].astype(BF16)) + b_ref[...]


def _ada_mod(cond16, w_mod, b_mod):
    n = N_MOD * D_MODEL
    tn = 1024
    return pl.pallas_call(
        _mod_kernel,
        grid=(n // tn,),
        in_specs=[pl.BlockSpec((16, D_MODEL), lambda j: (0, 0)),
                  pl.BlockSpec((D_MODEL, tn), lambda j: (0, j)),
                  pl.BlockSpec((1, tn), lambda j: (0, j))],
        out_specs=pl.BlockSpec((16, tn), lambda j: (0, j)),
        out_shape=jax.ShapeDtypeStruct((16, n), F32),
        compiler_params=_params(("arbitrary",)),
    )(cond16, w_mod, b_mod.reshape(1, n))


def _mod_row(b, t):
    return jnp.where(t == 0, 8, b)


def _modulated(x, g, shift, scale):
    ms = jnp.mean(x * x, axis=-1, keepdims=True)
    y = x * lax.rsqrt(ms + RMS_EPS) * g
    return y * (1.0 + scale) + shift


IN_SPLITS = (A_QKV, A_Z, LANES, B_QKV, C_QKV)
IN_COLS = sum(IN_SPLITS)


def _pair_sumsq_rsqrt(y, lo, eps, denom):
    yy = y * y
    s0 = jnp.sum(jnp.where(lo, yy, 0.0), axis=-1, keepdims=True)
    s1 = jnp.sum(jnp.where(lo, 0.0, yy), axis=-1, keepdims=True)
    return jnp.where(lo, lax.rsqrt(s0 * (1.0 / denom) + eps), lax.rsqrt(s1 * (1.0 / denom) + eps))


def _norm_rope(t, gain, cos, sin, lo, swap_lo):
    y = t * _pair_sumsq_rsqrt(t, lo, RMS_EPS, float(HEAD_DIM)) * gain
    sw = jnp.where(swap_lo, pltpu.roll(y, LANES - 16, 1), pltpu.roll(y, 16, 1))
    return y * cos + sw * sin


def _inproj_kernel(x_ref, mod_ref, g_ref, w_ref, cos_ref, sin_ref, gn_ref,
                   oa_ref, oz_ref, og_ref, bq_ref, bk_ref, bv_ref, cq_ref, ck_ref, cv_ref):
    m = mod_ref[0]
    h = _modulated(x_ref[0], g_ref[...], m[0:1], m[1:2]).astype(BF16)
    off = 0
    for o_ref, n in ((oa_ref, A_QKV), (oz_ref, A_Z), (og_ref, LANES)):
        o_ref[0] = _mm(h, w_ref[:, off:off + n])
        off += n
    lane = lax.broadcasted_iota(jnp.int32, (1, LANES), 1)
    lo = lane < HEAD_DIM
    swap_lo = (lane % 32) < 16
    cos = cos_ref[...]
    sin = sin_ref[...]
    gn = gn_ref[...]
    scale = HEAD_DIM ** -0.5
    for q_ref, k_ref, v_ref, gi in ((bq_ref, bk_ref, bv_ref, 0), (cq_ref, ck_ref, cv_ref, 2)):
        p = _mm(h, w_ref[:, off:off + B_QKV])
        off += B_QKV
        for j in range(2):
            t = p[:, j * LANES:(j + 1) * LANES]
            q_ref[0, :, j * LANES:(j + 1) * LANES] = (
                _norm_rope(t, gn[gi:gi + 1], cos, sin, lo, swap_lo) * scale).astype(BF16)
        k_ref[0] = _norm_rope(p[:, 2 * LANES:3 * LANES], gn[gi + 1:gi + 2], cos, sin, lo, swap_lo).astype(BF16)
        v_ref[0] = p[:, 3 * LANES:].astype(BF16)


def _in_proj(xt, mod, g, w, cos_t, sin_t, gains):
    bsz, tt, _ = xt.shape
    nt = tt // TOK_TILE
    tile = lambda n: pl.BlockSpec((1, TOK_TILE, n), lambda b, t: (b, t, 0))
    tab = pl.BlockSpec((TOK_TILE, LANES), lambda b, t: (t, 0))
    f32 = lambda n: jax.ShapeDtypeStruct((bsz, tt, n), F32)
    b16 = lambda n: jax.ShapeDtypeStruct((bsz, tt, n), BF16)
    return pl.pallas_call(
        _inproj_kernel,
        grid=(bsz, nt),
        in_specs=[tile(D_MODEL),
                  pl.BlockSpec((1, N_MOD, D_MODEL), lambda b, t: (_mod_row(b, t), 0, 0)),
                  pl.BlockSpec((1, D_MODEL), lambda b, t: (0, 0)),
                  pl.BlockSpec((D_MODEL, IN_COLS), lambda b, t: (0, 0)),
                  tab, tab, pl.BlockSpec((8, LANES), lambda b, t: (0, 0))],
        out_specs=[tile(A_QKV), tile(A_Z), tile(LANES)] + [tile(256), tile(LANES), tile(LANES)] * 2,
        out_shape=[f32(A_QKV), f32(A_Z), f32(LANES)] + [b16(256), b16(LANES), b16(LANES)] * 2,
        compiler_params=_params(("parallel", "parallel")),
    )(xt, mod, g, w, cos_t, sin_t, gains)


def _gdn_kernel(q_ref, k_ref, v_ref, gate_ref, z_ref, cwq_ref, cwk_ref, cwv_ref, alog_ref, dtb_ref,
                ng_ref, o_ref, qs, ks, vs, bs, gs, ts, grs, oacc, mq, bns, ext, *, n_ctx_chunks, n_chunks):
    hp = pl.program_id(1)
    tt = n_chunks * CHUNK
    lane = lax.broadcasted_iota(jnp.int32, (1, LANES), 1)
    lo = lane < HEAD_DIM
    bwd_lane = (lane % 4) >= 2
    r64 = lax.broadcasted_iota(jnp.int32, (CHUNK, CHUNK), 0)
    c64 = lax.broadcasted_iota(jnp.int32, (CHUNK, CHUNK), 1)
    tri = jnp.where(r64 >= c64, 1.0, 0.0).astype(BF16)
    shift = (LANES - 8 * hp) % LANES
    alog = alog_ref[...]
    dtb = dtb_ref[...]

    def prep(c, slot):
        r0 = pl.multiple_of(c * CHUNK, CHUNK)
        first = jnp.logical_or(c == 0, c == n_ctx_chunks)
        last = jnp.logical_or(c == n_ctx_chunks - 1, c == n_chunks - 1)
        for i, (src, cw_ref, dst, norm) in enumerate(((q_ref, cwq_ref, qs, True),
                                                      (k_ref, cwk_ref, ks, True),
                                                      (v_ref, cwv_ref, vs, False))):
            prev = src[0, pl.ds(jnp.maximum(r0 - 8, 0), 8), :]
            nxt = src[0, pl.ds(jnp.minimum(r0 + CHUNK, tt - 8), 8), :]
            ext[slot + i,0:8, :] = jnp.where(first, 0.0, prev)
            ext[slot + i,8:8 + CHUNK, :] = src[0, pl.ds(r0, CHUNK), :]
            ext[slot + i,8 + CHUNK:16 + CHUNK, :] = jnp.where(last, 0.0, nxt)
            cw = cw_ref[...]
            y = ext[slot + i,6:6 + CHUNK, :] * cw[0:1]
            for j in range(1, GDN_CONV):
                y = y + ext[slot + i,6 + j:6 + j + CHUNK, :] * cw[j:j + 1]
            y = y * _sigmoid(y)
            if norm:
                y = y * _pair_sumsq_rsqrt(y, lo, 1e-6, 1.0)
            dst[pl.ds(r0, CHUNK), :] = y
        gt = gate_ref[0, pl.ds(r0, CHUNK), :]
        beta = _sigmoid(gt)
        a = gt + dtb
        sp = jnp.maximum(a, 0.0) + jnp.log(1.0 + jnp.exp(-jnp.abs(a)))
        g = -jnp.exp(alog) * sp
        g_hi = g.astype(BF16)
        r1 = g - g_hi.astype(F32)
        g_mid = r1.astype(BF16)
        g_lo = (r1 - g_mid.astype(F32)).astype(BF16)
        gpre = _mm(tri, g_hi) + _mm(tri, g_mid) + _mm(tri, g_lo)
        tot = gpre[CHUNK - 1:CHUNK, :]
        gsuf = tot - gpre + g
        gc = jnp.where(bwd_lane, gsuf, gpre)
        bs[pl.ds(r0, CHUNK), :] = pltpu.roll(beta, shift, 1)
        gcr = pltpu.roll(gc, shift, 1)
        gs[pl.ds(r0, CHUNK), :] = gcr
        grs[pl.ds(pl.multiple_of(c * 8, 8), 8), :] = jnp.concatenate([gcr, gcr], axis=0).T[0:8, :]
        ts[pl.ds(pl.multiple_of(c * 8, 8), 8), :] = pltpu.roll(jnp.broadcast_to(tot, (8, LANES)), shift, 1)

    def prep_body(i, carry):
        for u in range(PREP_UNROLL):
            prep(i * PREP_UNROLL + u, 3 * u)
        return carry

    lax.fori_loop(0, n_chunks // PREP_UNROLL, prep_body, 0)

    ri = lax.broadcasted_iota(jnp.int32, (LANES, LANES), 0)
    ci = lax.broadcasted_iota(jnp.int32, (LANES, LANES), 1)
    bd = (ri < HEAD_DIM) == (ci < HEAD_DIM)
    ti = lax.broadcasted_iota(jnp.int32, (CHUNK, LANES), 0)
    tj = lax.broadcasted_iota(jnp.int32, (CHUNK, LANES), 1) % CHUNK
    blk8 = (ti // 8) == (tj // 8)
    merge_masks = [jnp.logical_and((ti // (2 * m)) == (tj // (2 * m)), (ti // m) != (tj // m))
                   for m in (8, 16, 32)]
    masks = [((ti >= tj), (ti > tj)), ((ti <= tj), (ti < tj))]

    def blockdiag(x):
        xb = x.astype(BF16)
        return jnp.where(bd, jnp.concatenate([xb, xb], axis=0), jnp.zeros((LANES, LANES), BF16))

    def chunk_par(c, d):
        incl, strict = masks[d]
        r0 = pl.multiple_of(c * CHUNK, CHUNK)
        ib, ig = 2 * d, 4 + 2 * d
        bt = bs[pl.ds(r0, CHUNK), :]
        gt = gs[pl.ds(r0, CHUNK), :]
        bcol = jnp.where(lo, bt[:, ib:ib + 1], bt[:, ib + 1:ib + 2])
        gcol = jnp.where(lo, gt[:, ig:ig + 1], gt[:, ig + 1:ig + 2])
        k = ks[pl.ds(r0, CHUNK), :]
        q = qs[pl.ds(r0, CHUNK), :] * (HEAD_DIM ** -0.5)
        x = _nt(jnp.concatenate([k * bcol, q], axis=0).astype(BF16), blockdiag(k))
        yield
        gr8 = grs[pl.ds(pl.multiple_of(c * 8, 8), 8), :]
        grow = jnp.where(lo, gr8[ig:ig + 1, :], gr8[ig + 1:ig + 2, :])
        dec = jnp.exp(jnp.where(incl, gcol - grow, NEG_BIG))
        low = jnp.where(strict, x[:CHUNK] * dec, 0.0)
        intra = (x[CHUNK:] * dec).astype(BF16)
        l0 = jnp.where(blk8, low, 0.0)
        p = _mm(l0.astype(BF16), blockdiag(l0))
        yield
        n = -l0
        y = _mm(jnp.concatenate([n, p], axis=0).astype(BF16), blockdiag(p))
        yield
        n = n + p + y[:CHUNK]
        p = y[CHUNK:]
        y = _mm(n.astype(BF16), blockdiag(p))
        yield
        n = n + p + y
        for cmask in merge_masks:
            cm = jnp.where(cmask, low, 0.0)
            y = _mm(n.astype(BF16), blockdiag(cm))
            yield
            tc = cm + y
            y = _mm(tc.astype(BF16), blockdiag(n))
            yield
            n = n - tc - y
        vb = vs[pl.ds(r0, CHUNK), :] * bcol
        kbg = k * bcol * jnp.exp(gcol)
        y = _mm(n.astype(BF16), jnp.concatenate([blockdiag(vb), blockdiag(kbg)], axis=1))
        yield
        u = vb + y[:, :LANES]
        w = kbg + y[:, LANES:]
        bu, bw = blockdiag(u), blockdiag(w)
        iuw = _mm(intra, jnp.concatenate([bu, bw], axis=1))
        tl8 = ts[pl.ds(pl.multiple_of(c * 8, 8), 8), :]
        tl = jnp.where(lo, tl8[0:1, ig:ig + 1], tl8[0:1, ig + 1:ig + 2])
        ke = k * jnp.exp(tl - gcol)
        kett = jnp.concatenate([ke, ke], axis=0).T
        ket = jnp.where(lo, kett[:CHUNK], kett[CHUNK:]).astype(BF16)
        mb = _mm(ket, jnp.concatenate([bw, bu], axis=1))
        yield
        qe = q * jnp.exp(gcol) - iuw[:, LANES:]
        return jnp.concatenate([qe, mb[:, :LANES]], axis=0).astype(BF16), mb[:, LANES:], iuw[:, :LANES]

    def interleave(gens):
        res = [None] * len(gens)
        live = list(range(len(gens)))
        while live:
            for idx in list(live):
                try:
                    next(gens[idx])
                except StopIteration as stop:
                    res[idx] = stop.value
                    live.remove(idx)
        return res

    def scan_chunk(p, d):
        if d == 0:
            return p
        return jnp.where(p < n_ctx_chunks, n_ctx_chunks - 1 - p, n_chunks + n_ctx_chunks - 1 - p)

    def seq_chain(g, d, s):
        ig = 4 + 2 * d
        for j in range(PAR_GROUP):
            c = scan_chunk(g * PAR_GROUP + j, d)
            r0 = pl.multiple_of(c * CHUNK, CHUNK)
            tl8 = ts[pl.ds(pl.multiple_of(c * 8, 8), 8), :]
            arow = jnp.exp(jnp.where(lo, tl8[0:1, ig:ig + 1], tl8[0:1, ig + 1:ig + 2]))
            y = _mm(mq[d, c], blockdiag(s))
            yield
            oacc[d, pl.ds(r0, CHUNK), :] = oacc[d, pl.ds(r0, CHUNK), :] + y[:CHUNK]
            s = arow * s - y[CHUNK:] + bns[d, c]
        return s

    def group_step(g, states, with_par):
        gens, cs = [], []
        if with_par:
            cs = [(scan_chunk(g * PAR_GROUP + j, d), d) for j in range(PAR_GROUP) for d in range(2)]
            gens = [chunk_par(c, d) for c, d in cs]
        if states is not None:
            gens += [seq_chain(g - 1, d, states[d]) for d in range(2)]
        res = interleave(gens)
        for (c, d), (qm, bn, iu) in zip(cs, res):
            mq[d, c] = qm
            bns[d, c] = bn
            oacc[d, pl.ds(pl.multiple_of(c * CHUNK, CHUNK), CHUNK), :] = iu
        return None if states is None else (res[-2], res[-1])

    n_groups = n_chunks // PAR_GROUP
    zero = jnp.zeros((CHUNK, LANES), F32)
    group_step(0, None, True)
    states = lax.fori_loop(1, n_groups, lambda g, st: group_step(g, st, True), (zero, zero))
    group_step(n_groups, states, False)

    ng = ng_ref[...]

    def fin(i, carry):
        for u in range(FIN_UNROLL):
            r0 = pl.multiple_of((i * FIN_UNROLL + u) * CHUNK, CHUNK)
            o = oacc[0, pl.ds(r0, CHUNK), :] + oacc[1, pl.ds(r0, CHUNK), :]
            z = z_ref[0, pl.ds(r0, CHUNK), :]
            y = o * _pair_sumsq_rsqrt(o, lo, RMS_EPS, float(HEAD_DIM)) * ng
            o_ref[0, pl.ds(r0, CHUNK), :] = (y * (z * _sigmoid(z))).astype(o_ref.dtype)
        return carry

    lax.fori_loop(0, n_chunks // FIN_UNROLL, fin, 0)


def _gdn(a_qkv, gates, a_z, cw8, alog_l, dtb_l, ng2, n_ctx):
    bsz, tt, _ = a_qkv.shape
    n_pairs = GDN_HEADS // 2
    tok = lambda off: pl.BlockSpec((1, tt, LANES), lambda b, h: (b, 0, h + off))
    par = lambda off: pl.BlockSpec((8, LANES), lambda b, h: (0, h + off))
    vec = pl.BlockSpec((1, LANES), lambda b, h: (0, 0))
    kern = functools.partial(_gdn_kernel, n_ctx_chunks=n_ctx // CHUNK, n_chunks=tt // CHUNK)
    return pl.pallas_call(
        kern,
        grid=(bsz, n_pairs),
        in_specs=[tok(0), tok(n_pairs), tok(2 * n_pairs),
                  pl.BlockSpec((1, tt, LANES), lambda b, h: (b, 0, 0)),
                  tok(0),
                  par(0), par(n_pairs), par(2 * n_pairs), vec, vec, vec],
        out_specs=tok(0),
        out_shape=jax.ShapeDtypeStruct((bsz, tt, A_Z), BF16),
        scratch_shapes=[pltpu.VMEM((tt, LANES), F32)] * 5
                       + [pltpu.VMEM((tt // CHUNK * 8, LANES), F32)] * 2
                       + [pltpu.VMEM((2, tt, LANES), F32),
                          pltpu.VMEM((2, tt // CHUNK, 2 * CHUNK, LANES), BF16),
                          pltpu.VMEM((2, tt // CHUNK, CHUNK, LANES), F32),
                          pltpu.VMEM((3 * PREP_UNROLL, CHUNK + 16, LANES), F32)],
        compiler_params=_params(("parallel", "parallel")),
    )(a_qkv, a_qkv, a_qkv, gates, a_z, cw8, cw8, cw8, alog_l, dtb_l, ng2)


def _head_queries(q_ref, lo):
    qms = []
    for tile in range(2):
        qt = q_ref[0, :, tile * LANES:(tile + 1) * LANES]
        for hh in range(2):
            qms.append(jnp.where(lo if hh == 0 else jnp.logical_not(lo), qt, jnp.zeros_like(qt)))
    return qms


def _store_heads(o_ref, outs, lo):
    for tile in range(2):
        o_ref[0, :, tile * LANES:(tile + 1) * LANES] = jnp.where(
            lo, outs[2 * tile], outs[2 * tile + 1]).astype(o_ref.dtype)


def _gattn_kernel(q_ref, k_ref, v_ref, o_ref, *, first_tile, n_ctx):
    t = pl.program_id(1) + first_tile
    lane = lax.broadcasted_iota(jnp.int32, (1, LANES), 1)
    lo = lane < HEAD_DIM

    def attend(kk, vv):
        scores = [_nt(qm, kk) for qm in _head_queries(q_ref, lo)]
        probs, sums = [], []
        for s in scores:
            p = jnp.exp(s - jnp.max(s, axis=-1, keepdims=True))
            sums.append(jnp.sum(p, axis=-1, keepdims=True))
            probs.append(p.astype(BF16))
        _store_heads(o_ref, [_mm(p, vv) / l for p, l in zip(probs, sums)], lo)

    if first_tile == 0:
        @pl.when(t == 0)
        def _():
            attend(k_ref[0, 0:n_ctx, :], v_ref[0, 0:n_ctx, :])

    @pl.when(t > 0)
    def _():
        attend(k_ref[0], v_ref[0])


def _global_attn(q, k, v, n_ctx, need_ctx):
    bsz, tt, _ = q.shape
    first = 0 if need_ctx else n_ctx // TOK_TILE
    nt = tt // TOK_TILE - first
    kv = pl.BlockSpec((1, tt, LANES), lambda b, t: (b, 0, 0))
    return pl.pallas_call(
        functools.partial(_gattn_kernel, first_tile=first, n_ctx=n_ctx),
        grid=(bsz, nt),
        in_specs=[pl.BlockSpec((1, TOK_TILE, 256), lambda b, t: (b, t + first, 0)), kv, kv],
        out_specs=pl.BlockSpec((1, TOK_TILE, 256), lambda b, t: (b, t, 0)),
        out_shape=jax.ShapeDtypeStruct((bsz, nt * TOK_TILE, 256), BF16),
        compiler_params=_params(("parallel", "arbitrary")),
    )(q, k, v)


def _wattn_kernel(q_ref, k_ref, v_ref, sink_ref, o_ref, *, first_tile, n_ctx, n_lat):
    t = pl.program_id(1) + first_tile
    n_ctx_tiles = n_ctx // WQ_TILE
    lane = lax.broadcasted_iota(jnp.int32, (1, LANES), 1)
    lo = lane < HEAD_DIM
    kw = WQ_TILE + 2 * WINDOW
    sink = sink_ref[...]
    sinks = [sink[2 * hh + tile:2 * hh + tile + 1, 0:1] for tile in range(2) for hh in range(2)]

    def attend(win):
        kc = k_ref[0, 0:n_ctx, :]
        vc = v_ref[0, 0:n_ctx, :]
        qms = _head_queries(q_ref, lo)
        s_ctx = [_nt(qm, kc) for qm in qms]
        mxs = [jnp.maximum(jnp.max(s, axis=-1, keepdims=True), sk) for s, sk in zip(s_ctx, sinks)]
        if win:
            n = t - n_ctx_tiles
            ws = jnp.clip(n * WQ_TILE - WINDOW, 0, n_lat - kw)
            start = pl.multiple_of(n_ctx + ws, WINDOW)
            kwn = k_ref[0, pl.ds(start, kw), :]
            vwn = v_ref[0, pl.ds(start, kw), :]
            qpos = n * WQ_TILE + lax.broadcasted_iota(jnp.int32, (WQ_TILE, kw), 0)
            kpos = ws + lax.broadcasted_iota(jnp.int32, (WQ_TILE, kw), 1)
            valid = jnp.abs(kpos - qpos) <= WINDOW
            s_win = [jnp.where(valid, _nt(qm, kwn), NEG_BIG) for qm in qms]
            mxs = [jnp.maximum(mx, jnp.max(s, axis=-1, keepdims=True)) for mx, s in zip(mxs, s_win)]
        p_ctx = [jnp.exp(s - mx) for s, mx in zip(s_ctx, mxs)]
        sums = [jnp.sum(p, axis=-1, keepdims=True) + jnp.exp(sk - mx) for p, sk, mx in zip(p_ctx, sinks, mxs)]
        outs = [_mm(p.astype(BF16), vc) for p in p_ctx]
        if win:
            p_win = [jnp.exp(s - mx) for s, mx in zip(s_win, mxs)]
            sums = [l + jnp.sum(p, axis=-1, keepdims=True) for l, p in zip(sums, p_win)]
            outs = [o + _mm(p.astype(BF16), vwn) for o, p in zip(outs, p_win)]
        _store_heads(o_ref, [o / l for o, l in zip(outs, sums)], lo)

    if first_tile == 0:
        @pl.when(t < n_ctx_tiles)
        def _():
            attend(False)

    @pl.when(t >= n_ctx_tiles)
    def _():
        attend(True)


def _window_attn(q, k, v, sink8, n_ctx, need_ctx):
    bsz, tt, _ = q.shape
    first = 0 if need_ctx else n_ctx // WQ_TILE
    nt = tt // WQ_TILE - first
    kv = pl.BlockSpec((1, tt, LANES), lambda b, t: (b, 0, 0))
    return pl.pallas_call(
        functools.partial(_wattn_kernel, first_tile=first, n_ctx=n_ctx, n_lat=tt - n_ctx),
        grid=(bsz, nt),
        in_specs=[pl.BlockSpec((1, WQ_TILE, 256), lambda b, t: (b, t + first, 0)), kv, kv,
                  pl.BlockSpec((8, LANES), lambda b, t: (0, 0))],
        out_specs=pl.BlockSpec((1, WQ_TILE, 256), lambda b, t: (b, t, 0)),
        out_shape=jax.ShapeDtypeStruct((bsz, nt * WQ_TILE, 256), BF16),
        compiler_params=_params(("parallel", "arbitrary")),
    )(q, k, v, sink8)


def _outmlp_kernel(x_ref, oa_ref, ob_ref, oc_ref, mod_ref, g_ref, wo_ref, w1_ref, w2_ref, o_ref):
    m = mod_ref[0]
    mix = _mm(oa_ref[0], wo_ref[0:A_Z, :])
    mix = mix + _mm(ob_ref[0], wo_ref[A_Z:A_Z + 256, :])
    mix = mix + _mm(oc_ref[0], wo_ref[A_Z + 256:A_Z + 512, :])
    x = x_ref[0] + m[2:3] * mix
    h = _modulated(x, g_ref[...], m[3:4], m[4:5]).astype(BF16)
    fc = 1024
    acc = None
    for f in range(D_FF // fc):
        a = jnp.maximum(_mm(h, w1_ref[:, f * fc:(f + 1) * fc]), 0.0)
        y = _mm((a * a).astype(BF16), w2_ref[f * fc:(f + 1) * fc, :])
        acc = y if acc is None else acc + y
    o_ref[0] = x + m[5:6] * acc


def _out_mlp(xt, oa, ob, oc, mod, g, wo, w1, w2, n_ctx, need_ctx):
    bsz, tt, _ = xt.shape
    first = 0 if need_ctx else n_ctx // TOK_TILE
    nt = tt // TOK_TILE - first
    tile = lambda n: pl.BlockSpec((1, TOK_TILE, n), lambda b, t: (b, t + first, 0))
    att = pl.BlockSpec((1, TOK_TILE, 256), lambda b, t: (b, t, 0))
    full = lambda r, c: pl.BlockSpec((r, c), lambda b, t: (0, 0))
    return pl.pallas_call(
        _outmlp_kernel,
        grid=(bsz, nt),
        in_specs=[tile(D_MODEL), tile(A_Z), att, att,
                  pl.BlockSpec((1, N_MOD, D_MODEL), lambda b, t: (_mod_row(b, t + first), 0, 0)),
                  full(1, D_MODEL), full(D_MODEL, D_MODEL), full(D_MODEL, D_FF), full(D_FF, D_MODEL)],
        out_specs=pl.BlockSpec((1, TOK_TILE, D_MODEL), lambda b, t: (b, t, 0)),
        out_shape=jax.ShapeDtypeStruct((bsz, nt * TOK_TILE, D_MODEL), F32),
        compiler_params=_params(("parallel", "parallel")),
    )(xt, oa, ob, oc, mod, g, wo, w1, w2)


_HEAD_PERM = (0, 2, 1, 3)


def _perm_cols(n_heads_q):
    cols = []
    for h in _HEAD_PERM:
        cols.extend(range(h * HEAD_DIM, (h + 1) * HEAD_DIM))
    return cols


def _gate_cols():
    base_beta = A_QKV + A_Z
    base_alpha = base_beta + A_GATES
    cols = []
    for pair in range(GDN_HEADS // 2):
        for base in (base_beta, base_alpha):
            for d in range(2):
                for j in range(2):
                    cols.append(base + d * GDN_HEADS + 2 * pair + j)
    return cols


def _gate_param_lanes(p):
    out = jnp.zeros((LANES,), F32)
    idx, src = [], []
    for pair in range(GDN_HEADS // 2):
        for d in range(2):
            for j in range(2):
                idx.append(pair * 8 + 4 + d * 2 + j)
                src.append(d * GDN_HEADS + 2 * pair + j)
    return out.at[jnp.array(idx)].set(p.reshape(-1)[jnp.array(src)]).reshape(1, LANES)


def _rope_tables(n_ctx, n_lat):
    rows = n_lat // GRID_W
    row = jnp.repeat(jnp.arange(rows, dtype=F32), GRID_W)
    col = jnp.tile(jnp.arange(GRID_W, dtype=F32), rows)
    half = HEAD_DIM // 4
    inv_freq = ROPE_THETA ** (-jnp.arange(half, dtype=F32) / half)
    ang_r = row[:, None] * inv_freq
    ang_c = col[:, None] * inv_freq
    cr, sr, cc, sc = jnp.cos(ang_r), jnp.sin(ang_r), jnp.cos(ang_c), jnp.sin(ang_c)
    cos = jnp.concatenate([cr, cr, cc, cc], axis=-1)
    sin = jnp.concatenate([-sr, sr, -sc, sc], axis=-1)
    cos = jnp.concatenate([jnp.ones((n_ctx, HEAD_DIM), F32), cos], axis=0)
    sin = jnp.concatenate([jnp.zeros((n_ctx, HEAD_DIM), F32), sin], axis=0)
    return jnp.tile(cos, (1, 2)), jnp.tile(sin, (1, 2))


def _pad_rows(a, rows):
    return jnp.concatenate([a, jnp.zeros((rows - a.shape[0],) + a.shape[1:], a.dtype)], axis=0)


def kernel(x, c, ctx, c_ctx, w_mod, b_mod, g_attn, w_in, gdn_conv_w, gdn_a_log, gdn_dt_bias, gdn_norm_g,
           ga_q_norm_g, ga_k_norm_g, wa_q_norm_g, wa_k_norm_g, wa_sink, w_out, g_mlp, w_mlp_in, w_mlp_out):
    bsz, n_lat, _ = x.shape
    n_ctx = ctx.shape[1]
    depth = w_mod.shape[0]
    assert bsz <= 8 and n_ctx == TOK_TILE and n_lat % TOK_TILE == 0

    xt = jnp.concatenate([ctx, x], axis=1)
    cond16 = _pad_rows(jnp.concatenate([_pad_rows(c, 8), c_ctx[None, :]], axis=0), 16)
    cos_t, sin_t = _rope_tables(n_ctx, n_lat)

    b0 = A_QKV + A_Z + 2 * A_GATES
    c0 = b0 + B_QKV
    qperm = _perm_cols(GA_HEADS)
    in_cols = (list(range(A_QKV + A_Z)) + _gate_cols()
               + [b0 + i for i in qperm] + list(range(b0 + 256, b0 + B_QKV))
               + [c0 + i for i in qperm] + list(range(c0 + 256, c0 + C_QKV)))
    in_cols = np.array(in_cols)
    gate_pad = jnp.zeros((D_MODEL, LANES - 2 * A_GATES), BF16)
    out_rows = np.array(list(range(A_Z)) + [A_Z + i for i in qperm] + [A_Z + 256 + i for i in qperm])

    for l in range(depth):
        need_ctx = l < depth - 1
        mod = _ada_mod(cond16, w_mod[l], b_mod[l]).reshape(16, N_MOD, D_MODEL)
        wl = w_in[l].astype(BF16)[:, in_cols]
        n_front = A_QKV + A_Z + 2 * A_GATES
        w_in_l = jnp.concatenate([wl[:, :n_front], gate_pad, wl[:, n_front:]], axis=1)
        gains = _pad_rows(jnp.stack([jnp.tile(g, 2) for g in
                                     (ga_q_norm_g[l], ga_k_norm_g[l], wa_q_norm_g[l], wa_k_norm_g[l])]), 8)
        a_qkv, a_z, gates, bq, bk, bv, cq, ck, cv = _in_proj(xt, mod, g_attn[l].reshape(1, D_MODEL), w_in_l,
                                                            cos_t, sin_t, gains)

        o_a = _gdn(a_qkv, gates, a_z, _pad_rows(gdn_conv_w[l], 8), _gate_param_lanes(gdn_a_log[l]),
                   _gate_param_lanes(gdn_dt_bias[l]), jnp.tile(gdn_norm_g[l], 2).reshape(1, LANES), n_ctx)

        o_b = _global_attn(bq, bk, bv, n_ctx, need_ctx)
        sink8 = _pad_rows(jnp.broadcast_to(wa_sink[l][:, None], (WA_HEADS, LANES)), 8)
        o_c = _window_attn(cq, ck, cv, sink8, n_ctx, need_ctx)

        xt = _out_mlp(xt, o_a, o_b, o_c, mod, g_mlp[l].reshape(1, D_MODEL),
                      w_out[l].astype(BF16)[out_rows, :], w_mlp_in[l].astype(BF16),
                      w_mlp_out[l].astype(BF16), n_ctx, need_ctx)
    return xt
```

```python
import functools
import math

import numpy as np
import jax
import jax.numpy as jnp
from jax import lax
from jax.experimental import pallas as pl
from jax.experimental.pallas import tpu as pltpu

F32 = jnp.float32
BF16 = jnp.bfloat16

D_MODEL = 1024
GRID_W = 64
HEAD_DIM = 64
ROPE_THETA = 10000.0
GDN_HEADS = 8
GDN_CONV = 5
CHUNK = 64
GA_HEADS = 4
WA_HEADS = 4
WINDOW = 128
D_FF = 4 * D_MODEL
N_MOD = 6
RMS_EPS = 1e-6
A_QKV = 1536
A_Z = 512
A_GATES = 16
B_QKV = 512
C_QKV = 512
LANES = 128
TOK_TILE = 256
WQ_TILE = 256
PREP_UNROLL = 4
FIN_UNROLL = 4
PAR_GROUP = 9
NEG_BIG = -1e30
VMEM_LIMIT = 56 * 1024 * 1024


def _mm(a, b):
    return jnp.dot(a, b, preferred_element_type=F32)


def _nt(a, b):
    return lax.dot_general(a, b, (((1,), (1,)), ((), ())), preferred_element_type=F32)


def _sigmoid(x):
    return 1.0 / (1.0 + jnp.exp(-x))


def _params(sem):
    return pltpu.CompilerParams(dimension_semantics=sem, vmem_limit_bytes=VMEM_LIMIT)


def _mod_kernel(c_ref, w_ref, b_ref, o_ref):
    c = c_ref[...]
    s = c * _sigmoid(c)
    o_ref[...] = _mm(s.astype(BF16), w_ref[...].astype(BF16)) + b_ref[...]


def _ada_mod(cond16, w_mod, b_mod):
    n = N_MOD * D_MODEL
    tn = 1024
    return pl.pallas_call(
        _mod_kernel,
        grid=(n // tn,),
        in_specs=[pl.BlockSpec((16, D_MODEL), lambda j: (0, 0)),
                  pl.BlockSpec((D_MODEL, tn), lambda j: (0, j)),
                  pl.BlockSpec((1, tn), lambda j: (0, j))],
        out_specs=pl.BlockSpec((16, tn), lambda j: (0, j)),
        out_shape=jax.ShapeDtypeStruct((16, n), F32),
        compiler_params=_params(("arbitrary",)),
    )(cond16, w_mod, b_mod.reshape(1, n))


def _mod_row(b, t):
    return jnp.where(t == 0, 8, b)


def _modulated(x, g, shift, scale):
    ms = jnp.mean(x * x, axis=-1, keepdims=True)
    y = x * lax.rsqrt(ms + RMS_EPS) * g
    return y * (1.0 + scale) + shift


IN_SPLITS = (A_QKV, A_Z, LANES, B_QKV, C_QKV)
IN_COLS = sum(IN_SPLITS)


def _pair_sumsq_rsqrt(y, lo, eps, denom):
    yy = y * y
    s0 = jnp.sum(jnp.where(lo, yy, 0.0), axis=-1, keepdims=True)
    s1 = jnp.sum(jnp.where(lo, 0.0, yy), axis=-1, keepdims=True)
    return jnp.where(lo, lax.rsqrt(s0 * (1.0 / denom) + eps), lax.rsqrt(s1 * (1.0 / denom) + eps))


def _norm_rope(t, gain, cos, sin, lo, swap_lo):
    y = t * _pair_sumsq_rsqrt(t, lo, RMS_EPS, float(HEAD_DIM)) * gain
    sw = jnp.where(swap_lo, pltpu.roll(y, LANES - 16, 1), pltpu.roll(y, 16, 1))
    return y * cos + sw * sin


HALO = 8


def _inproj_kernel(xp_ref, x_ref, xn_ref, mod_ref, g_ref, w_ref, cw_ref, alog_ref, dtb_ref, cos_ref, sin_ref,
                   gn_ref, oa_ref, oz_ref, ob_ref, og_ref, ot_ref, bq_ref, bk_ref, bv_ref, cq_ref, ck_ref, cv_ref,
                   ext, *, n_ctx_tiles):
    t = pl.program_id(1)
    nt = pl.num_programs(1)
    m = mod_ref[0]
    g = g_ref[...]
    xc = _modulated(x_ref[0], g, m[0:1], m[1:2])
    h = xc.astype(BF16)
    lane = lax.broadcasted_iota(jnp.int32, (1, LANES), 1)
    lo = lane < HEAD_DIM

    hx = jnp.concatenate([_modulated(xp_ref[0], g, m[0:1], m[1:2]), xc,
                          _modulated(xn_ref[0], g, m[0:1], m[1:2])], axis=0).astype(BF16)
    keep_prev = jnp.where(jnp.logical_and(t != 0, t != n_ctx_tiles), 1.0, 0.0)
    keep_next = jnp.where(jnp.logical_and(t != n_ctx_tiles - 1, t != nt - 1), 1.0, 0.0)
    rowi = lax.broadcasted_iota(jnp.int32, (TOK_TILE + 2 * HALO, 1), 0)
    keep = jnp.where(rowi < HALO, keep_prev, jnp.where(rowi >= TOK_TILE + HALO, keep_next, 1.0))
    gw = 2 * LANES
    for j in range(A_QKV // gw):
        ext[j] = _mm(hx, w_ref[:, j * gw:(j + 1) * gw]) * keep
        cw = cw_ref[:, j * gw:(j + 1) * gw]
        base = HALO - GDN_CONV // 2
        y = ext[j, base:base + TOK_TILE, :] * cw[0:1]
        for tap in range(1, GDN_CONV):
            y = y + ext[j, base + tap:base + tap + TOK_TILE, :] * cw[tap:tap + 1]
        y = y * _sigmoid(y)
        for half in range(2):
            yh = y[:, half * LANES:(half + 1) * LANES]
            if j * gw < 2 * (A_QKV // 3):
                yh = yh * _pair_sumsq_rsqrt(yh, lo, 1e-6, 1.0)
            oa_ref[0, :, j * gw + half * LANES:j * gw + (half + 1) * LANES] = yh.astype(BF16)
    off = A_QKV
    oz_ref[0] = _mm(h, w_ref[:, off:off + A_Z])
    off += A_Z

    gt = _mm(h, w_ref[:, off:off + LANES])
    off += LANES
    a = gt + dtb_ref[...]
    sp = jnp.maximum(a, 0.0) + jnp.log(1.0 + jnp.exp(-jnp.abs(a)))
    gl = -jnp.exp(alog_ref[...]) * sp
    g_hi = gl.astype(BF16)
    r1 = gl - g_hi.astype(F32)
    g_mid = r1.astype(BF16)
    g_lo = (r1 - g_mid.astype(F32)).astype(BF16)
    ri = lax.broadcasted_iota(jnp.int32, (TOK_TILE, TOK_TILE), 0)
    ci = lax.broadcasted_iota(jnp.int32, (TOK_TILE, TOK_TILE), 1)
    same = (ri // CHUNK) == (ci // CHUNK)
    blk = jnp.where(same, 1.0, 0.0).astype(BF16)
    tri = jnp.where(jnp.logical_and(same, ri >= ci), 1.0, 0.0).astype(BF16)
    gpre = _mm(tri, g_hi) + _mm(tri, g_mid) + _mm(tri, g_lo)
    tot = _mm(blk, g_hi) + _mm(blk, g_mid) + _mm(blk, g_lo)
    ob_ref[0] = _sigmoid(gt)
    og_ref[0] = jnp.where((lane % 4) >= 2, tot - gpre + gl, gpre)
    ot_ref[0] = tot

    swap_lo = (lane % 32) < 16
    cos = cos_ref[...]
    sin = sin_ref[...]
    gn = gn_ref[...]
    scale = HEAD_DIM ** -0.5
    for q_ref, k_ref, v_ref, gi in ((bq_ref, bk_ref, bv_ref, 0), (cq_ref, ck_ref, cv_ref, 2)):
        p = _mm(h, w_ref[:, off:off + B_QKV])
        off += B_QKV
        for j in range(2):
            t = p[:, j * LANES:(j + 1) * LANES]
            q_ref[0, :, j * LANES:(j + 1) * LANES] = (
                _norm_rope(t, gn[gi:gi + 1], cos, sin, lo, swap_lo) * scale).astype(BF16)
        k_ref[0] = _norm_rope(p[:, 2 * LANES:3 * LANES], gn[gi + 1:gi + 2], cos, sin, lo, swap_lo).astype(BF16)
        v_ref[0] = p[:, 3 * LANES:].astype(BF16)


def _in_proj(xt, mod, g, w, cw8, alog_l, dtb_l, cos_t, sin_t, gains, n_ctx):
    bsz, tt, _ = xt.shape
    nt = tt // TOK_TILE
    per = TOK_TILE // HALO
    tile = lambda n: pl.BlockSpec((1, TOK_TILE, n), lambda b, t: (b, t, 0))
    tab = pl.BlockSpec((TOK_TILE, LANES), lambda b, t: (t, 0))
    vec = pl.BlockSpec((1, LANES), lambda b, t: (0, 0))
    f32 = lambda n: jax.ShapeDtypeStruct((bsz, tt, n), F32)
    b16 = lambda n: jax.ShapeDtypeStruct((bsz, tt, n), BF16)
    return pl.pallas_call(
        functools.partial(_inproj_kernel, n_ctx_tiles=n_ctx // TOK_TILE),
        grid=(bsz, nt),
        in_specs=[pl.BlockSpec((1, HALO, D_MODEL), lambda b, t: (b, jnp.maximum(t * per - 1, 0), 0)),
                  tile(D_MODEL),
                  pl.BlockSpec((1, HALO, D_MODEL), lambda b, t: (b, jnp.minimum((t + 1) * per, nt * per - 1), 0)),
                  pl.BlockSpec((1, N_MOD, D_MODEL), lambda b, t: (_mod_row(b, t), 0, 0)),
                  pl.BlockSpec((1, D_MODEL), lambda b, t: (0, 0)),
                  pl.BlockSpec((D_MODEL, IN_COLS), lambda b, t: (0, 0)),
                  pl.BlockSpec((8, A_QKV), lambda b, t: (0, 0)), vec, vec,
                  tab, tab, pl.BlockSpec((8, LANES), lambda b, t: (0, 0))],
        out_specs=[tile(A_QKV), tile(A_Z), tile(LANES), tile(LANES), tile(LANES)]
                  + [tile(256), tile(LANES), tile(LANES)] * 2,
        out_shape=[b16(A_QKV), f32(A_Z), f32(LANES), f32(LANES), f32(LANES)]
                  + [b16(256), b16(LANES), b16(LANES)] * 2,
        scratch_shapes=[pltpu.VMEM((A_QKV // (2 * LANES), TOK_TILE + 2 * HALO, 2 * LANES), F32)],
        compiler_params=_params(("parallel", "parallel")),
    )(xt, xt, xt, mod, g, w, cw8, alog_l, dtb_l, cos_t, sin_t, gains)


def _gdn_kernel(q_ref, k_ref, v_ref, beta_ref, gc_ref, tot_ref, z_ref, ng_ref, o_ref,
                bs, gs, ts, grs, oacc, mq, bns, *, n_ctx_chunks, n_chunks):
    hp = pl.program_id(1)
    lane = lax.broadcasted_iota(jnp.int32, (1, LANES), 1)
    lo = lane < HEAD_DIM
    shift = (LANES - 8 * hp) % LANES

    def prep_body(i, carry):
        for u in range(PREP_UNROLL):
            c = i * PREP_UNROLL + u
            r0 = pl.multiple_of(c * CHUNK, CHUNK)
            c8 = pl.multiple_of(c * 8, 8)
            bs[pl.ds(r0, CHUNK), :] = pltpu.roll(beta_ref[0, pl.ds(r0, CHUNK), :], shift, 1)
            gcr = pltpu.roll(gc_ref[0, pl.ds(r0, CHUNK), :], shift, 1)
            gs[pl.ds(r0, CHUNK), :] = gcr
            grs[pl.ds(c8, 8), :] = jnp.concatenate([gcr, gcr], axis=0).T[0:8, :]
            ts[pl.ds(c8, 8), :] = pltpu.roll(tot_ref[0, pl.ds(r0, 8), :], shift, 1)
        return carry

    lax.fori_loop(0, n_chunks // PREP_UNROLL, prep_body, 0)

    ri = lax.broadcasted_iota(jnp.int32, (LANES, LANES), 0)
    ci = lax.broadcasted_iota(jnp.int32, (LANES, LANES), 1)
    bd = (ri < HEAD_DIM) == (ci < HEAD_DIM)
    ti = lax.broadcasted_iota(jnp.int32, (CHUNK, LANES), 0)
    tj = lax.broadcasted_iota(jnp.int32, (CHUNK, LANES), 1) % CHUNK
    blk8 = (ti // 8) == (tj // 8)
    merge_masks = [jnp.logical_and((ti // (2 * m)) == (tj // (2 * m)), (ti // m) != (tj // m))
                   for m in (8, 16, 32)]
    masks = [((ti >= tj), (ti > tj)), ((ti <= tj), (ti < tj))]

    def blockdiag(x):
        xb = x.astype(BF16)
        return jnp.where(bd, jnp.concatenate([xb, xb], axis=0), jnp.zeros((LANES, LANES), BF16))

    def chunk_par(c, d):
        incl, strict = masks[d]
        r0 = pl.multiple_of(c * CHUNK, CHUNK)
        ib, ig = 2 * d, 4 + 2 * d
        bt = bs[pl.ds(r0, CHUNK), :]
        gt = gs[pl.ds(r0, CHUNK), :]
        bcol = jnp.where(lo, bt[:, ib:ib + 1], bt[:, ib + 1:ib + 2])
        gcol = jnp.where(lo, gt[:, ig:ig + 1], gt[:, ig + 1:ig + 2])
        k = k_ref[0, pl.ds(r0, CHUNK), :].astype(F32)
        q = q_ref[0, pl.ds(r0, CHUNK), :].astype(F32) * (HEAD_DIM ** -0.5)
        x = _nt(jnp.concatenate([k * bcol, q], axis=0).astype(BF16), blockdiag(k))
        yield
        gr8 = grs[pl.ds(pl.multiple_of(c * 8, 8), 8), :]
        grow = jnp.where(lo, gr8[ig:ig + 1, :], gr8[ig + 1:ig + 2, :])
        dec = jnp.exp(jnp.where(incl, gcol - grow, NEG_BIG))
        low = jnp.where(strict, x[:CHUNK] * dec, 0.0)
        intra = (x[CHUNK:] * dec).astype(BF16)
        l0 = jnp.where(blk8, low, 0.0)
        p = _mm(l0.astype(BF16), blockdiag(l0))
        yield
        n = -l0
        y = _mm(jnp.concatenate([n, p], axis=0).astype(BF16), blockdiag(p))
        yield
        n = n + p + y[:CHUNK]
        p = y[CHUNK:]
        y = _mm(n.astype(BF16), blockdiag(p))
        yield
        n = n + p + y
        for cmask in merge_masks:
            cm = jnp.where(cmask, low, 0.0)
            y = _mm(n.astype(BF16), blockdiag(cm))
            yield
            tc = cm + y
            y = _mm(tc.astype(BF16), blockdiag(n))
            yield
            n = n - tc - y
        vb = v_ref[0, pl.ds(r0, CHUNK), :].astype(F32) * bcol
        kbg = k * bcol * jnp.exp(gcol)
        y = _mm(n.astype(BF16), jnp.concatenate([blockdiag(vb), blockdiag(kbg)], axis=1))
        yield
        u = vb + y[:, :LANES]
        w = kbg + y[:, LANES:]
        bu, bw = blockdiag(u), blockdiag(w)
        iuw = _mm(intra, jnp.concatenate([bu, bw], axis=1))
        tl8 = ts[pl.ds(pl.multiple_of(c * 8, 8), 8), :]
        tl = jnp.where(lo, tl8[0:1, ig:ig + 1], tl8[0:1, ig + 1:ig + 2])
        ke = k * jnp.exp(tl - gcol)
        kett = jnp.concatenate([ke, ke], axis=0).T
        ket = jnp.where(lo, kett[:CHUNK], kett[CHUNK:]).astype(BF16)
        mb = _mm(ket, jnp.concatenate([bw, bu], axis=1))
        yield
        qe = q * jnp.exp(gcol) - iuw[:, LANES:]
        return jnp.concatenate([qe, mb[:, :LANES]], axis=0).astype(BF16), mb[:, LANES:], iuw[:, :LANES]

    def interleave(gens):
        res = [None] * len(gens)
        live = list(range(len(gens)))
        while live:
            for idx in list(live):
                try:
                    next(gens[idx])
                except StopIteration as stop:
                    res[idx] = stop.value
                    live.remove(idx)
        return res

    def scan_chunk(p, d):
        if d == 0:
            return p
        return jnp.where(p < n_ctx_chunks, n_ctx_chunks - 1 - p, n_chunks + n_ctx_chunks - 1 - p)

    def seq_chain(g, d, s):
        ig = 4 + 2 * d
        for j in range(PAR_GROUP):
            c = scan_chunk(g * PAR_GROUP + j, d)
            r0 = pl.multiple_of(c * CHUNK, CHUNK)
            tl8 = ts[pl.ds(pl.multiple_of(c * 8, 8), 8), :]
            arow = jnp.exp(jnp.where(lo, tl8[0:1, ig:ig + 1], tl8[0:1, ig + 1:ig + 2]))
            y = _mm(mq[d, c], blockdiag(s))
            yield
            oacc[d, pl.ds(r0, CHUNK), :] = oacc[d, pl.ds(r0, CHUNK), :] + y[:CHUNK]
            s = arow * s - y[CHUNK:] + bns[d, c]
        return s

    def group_step(g, states, with_par):
        gens, cs = [], []
        if with_par:
            cs = [(scan_chunk(g * PAR_GROUP + j, d), d) for j in range(PAR_GROUP) for d in range(2)]
            gens = [chunk_par(c, d) for c, d in cs]
        if states is not None:
            gens += [seq_chain(g - 1, d, states[d]) for d in range(2)]
        res = interleave(gens)
        for (c, d), (qm, bn, iu) in zip(cs, res):
            mq[d, c] = qm
            bns[d, c] = bn
            oacc[d, pl.ds(pl.multiple_of(c * CHUNK, CHUNK), CHUNK), :] = iu
        return None if states is None else (res[-2], res[-1])

    n_groups = n_chunks // PAR_GROUP
    zero = jnp.zeros((CHUNK, LANES), F32)
    group_step(0, None, True)
    states = lax.fori_loop(1, n_groups, lambda g, st: group_step(g, st, True), (zero, zero))
    group_step(n_groups, states, False)

    ng = ng_ref[...]

    def fin(i, carry):
        for u in range(FIN_UNROLL):
            r0 = pl.multiple_of((i * FIN_UNROLL + u) * CHUNK, CHUNK)
            o = oacc[0, pl.ds(r0, CHUNK), :] + oacc[1, pl.ds(r0, CHUNK), :]
            z = z_ref[0, pl.ds(r0, CHUNK), :]
            y = o * _pair_sumsq_rsqrt(o, lo, RMS_EPS, float(HEAD_DIM)) * ng
            o_ref[0, pl.ds(r0, CHUNK), :] = (y * (z * _sigmoid(z))).astype(o_ref.dtype)
        return carry

    lax.fori_loop(0, n_chunks // FIN_UNROLL, fin, 0)


def _gdn(a_qkv, beta, gcum, gtot, a_z, ng2, n_ctx):
    bsz, tt, _ = a_qkv.shape
    n_pairs = GDN_HEADS // 2
    tok = lambda off: pl.BlockSpec((1, tt, LANES), lambda b, h: (b, 0, h + off))
    gate = pl.BlockSpec((1, tt, LANES), lambda b, h: (b, 0, 0))
    kern = functools.partial(_gdn_kernel, n_ctx_chunks=n_ctx // CHUNK, n_chunks=tt // CHUNK)
    return pl.pallas_call(
        kern,
        grid=(bsz, n_pairs),
        in_specs=[tok(0), tok(n_pairs), tok(2 * n_pairs), gate, gate, gate, tok(0),
                  pl.BlockSpec((1, LANES), lambda b, h: (0, 0))],
        out_specs=tok(0),
        out_shape=jax.ShapeDtypeStruct((bsz, tt, A_Z), BF16),
        scratch_shapes=[pltpu.VMEM((tt, LANES), F32)] * 2
                       + [pltpu.VMEM((tt // CHUNK * 8, LANES), F32)] * 2
                       + [pltpu.VMEM((2, tt, LANES), F32),
                          pltpu.VMEM((2, tt // CHUNK, 2 * CHUNK, LANES), BF16),
                          pltpu.VMEM((2, tt // CHUNK, CHUNK, LANES), F32)],
        compiler_params=_params(("parallel", "parallel")),
    )(a_qkv, a_qkv, a_qkv, beta, gcum, gtot, a_z, ng2)


def _head_queries(q_ref, lo):
    qms = []
    for tile in range(2):
        qt = q_ref[0, :, tile * LANES:(tile + 1) * LANES]
        for hh in range(2):
            qms.append(jnp.where(lo if hh == 0 else jnp.logical_not(lo), qt, jnp.zeros_like(qt)))
    return qms


def _store_heads(o_ref, outs, lo):
    for tile in range(2):
        o_ref[0, :, tile * LANES:(tile + 1) * LANES] = jnp.where(
            lo, outs[2 * tile], outs[2 * tile + 1]).astype(o_ref.dtype)


def _gattn_kernel(q_ref, k_ref, v_ref, o_ref, *, first_tile, n_ctx):
    t = pl.program_id(1) + first_tile
    lane = lax.broadcasted_iota(jnp.int32, (1, LANES), 1)
    lo = lane < HEAD_DIM

    def attend(kk, vv):
        scores = [_nt(qm, kk) for qm in _head_queries(q_ref, lo)]
        probs, sums = [], []
        for s in scores:
            p = jnp.exp(s - jnp.max(s, axis=-1, keepdims=True))
            sums.append(jnp.sum(p, axis=-1, keepdims=True))
            probs.append(p.astype(BF16))
        _store_heads(o_ref, [_mm(p, vv) / l for p, l in zip(probs, sums)], lo)

    if first_tile == 0:
        @pl.when(t == 0)
        def _():
            attend(k_ref[0, 0:n_ctx, :], v_ref[0, 0:n_ctx, :])

    @pl.when(t > 0)
    def _():
        attend(k_ref[0], v_ref[0])


def _global_attn(q, k, v, n_ctx, need_ctx):
    bsz, tt, _ = q.shape
    first = 0 if need_ctx else n_ctx // TOK_TILE
    nt = tt // TOK_TILE - first
    kv = pl.BlockSpec((1, tt, LANES), lambda b, t: (b, 0, 0))
    return pl.pallas_call(
        functools.partial(_gattn_kernel, first_tile=first, n_ctx=n_ctx),
        grid=(bsz, nt),
        in_specs=[pl.BlockSpec((1, TOK_TILE, 256), lambda b, t: (b, t + first, 0)), kv, kv],
        out_specs=pl.BlockSpec((1, TOK_TILE, 256), lambda b, t: (b, t, 0)),
        out_shape=jax.ShapeDtypeStruct((bsz, nt * TOK_TILE, 256), BF16),
        compiler_params=_params(("parallel", "arbitrary")),
    )(q, k, v)


def _wattn_kernel(q_ref, k_ref, v_ref, sink_ref, o_ref, *, first_tile, n_ctx, n_lat):
    t = pl.program_id(1) + first_tile
    n_ctx_tiles = n_ctx // WQ_TILE
    lane = lax.broadcasted_iota(jnp.int32, (1, LANES), 1)
    lo = lane < HEAD_DIM
    kw = WQ_TILE + 2 * WINDOW
    sink = sink_ref[...]
    sinks = [sink[2 * hh + tile:2 * hh + tile + 1, 0:1] for tile in range(2) for hh in range(2)]

    def attend(win):
        kc = k_ref[0, 0:n_ctx, :]
        vc = v_ref[0, 0:n_ctx, :]
        qms = _head_queries(q_ref, lo)
        s_ctx = [_nt(qm, kc) for qm in qms]
        mxs = [jnp.maximum(jnp.max(s, axis=-1, keepdims=True), sk) for s, sk in zip(s_ctx, sinks)]
        if win:
            n = t - n_ctx_tiles
            ws = jnp.clip(n * WQ_TILE - WINDOW, 0, n_lat - kw)
            start = pl.multiple_of(n_ctx + ws, WINDOW)
            kwn = k_ref[0, pl.ds(start, kw), :]
            vwn = v_ref[0, pl.ds(start, kw), :]
            qpos = n * WQ_TILE + lax.broadcasted_iota(jnp.int32, (WQ_TILE, kw), 0)
            kpos = ws + lax.broadcasted_iota(jnp.int32, (WQ_TILE, kw), 1)
            valid = jnp.abs(kpos - qpos) <= WINDOW
            s_win = [jnp.where(valid, _nt(qm, kwn), NEG_BIG) for qm in qms]
            mxs = [jnp.maximum(mx, jnp.max(s, axis=-1, keepdims=True)) for mx, s in zip(mxs, s_win)]
        p_ctx = [jnp.exp(s - mx) for s, mx in zip(s_ctx, mxs)]
        sums = [jnp.sum(p, axis=-1, keepdims=True) + jnp.exp(sk - mx) for p, sk, mx in zip(p_ctx, sinks, mxs)]
        outs = [_mm(p.astype(BF16), vc) for p in p_ctx]
        if win:
            p_win = [jnp.exp(s - mx) for s, mx in zip(s_win, mxs)]
            sums = [l + jnp.sum(p, axis=-1, keepdims=True) for l, p in zip(sums, p_win)]
            outs = [o + _mm(p.astype(BF16), vwn) for o, p in zip(outs, p_win)]
        _store_heads(o_ref, [o / l for o, l in zip(outs, sums)], lo)

    if first_tile == 0:
        @pl.when(t < n_ctx_tiles)
        def _():
            attend(False)

    @pl.when(t >= n_ctx_tiles)
    def _():
        attend(True)


def _window_attn(q, k, v, sink8, n_ctx, need_ctx):
    bsz, tt, _ = q.shape
    first = 0 if need_ctx else n_ctx // WQ_TILE
    nt = tt // WQ_TILE - first
    kv = pl.BlockSpec((1, tt, LANES), lambda b, t: (b, 0, 0))
    return pl.pallas_call(
        functools.partial(_wattn_kernel, first_tile=first, n_ctx=n_ctx, n_lat=tt - n_ctx),
        grid=(bsz, nt),
        in_specs=[pl.BlockSpec((1, WQ_TILE, 256), lambda b, t: (b, t + first, 0)), kv, kv,
                  pl.BlockSpec((8, LANES), lambda b, t: (0, 0))],
        out_specs=pl.BlockSpec((1, WQ_TILE, 256), lambda b, t: (b, t, 0)),
        out_shape=jax.ShapeDtypeStruct((bsz, nt * WQ_TILE, 256), BF16),
        compiler_params=_params(("parallel", "arbitrary")),
    )(q, k, v, sink8)


def _outmlp_kernel(x_ref, oa_ref, ob_ref, oc_ref, mod_ref, g_ref, wo_ref, w1_ref, w2_ref, o_ref):
    m = mod_ref[0]
    mix = _mm(oa_ref[0], wo_ref[0:A_Z, :])
    mix = mix + _mm(ob_ref[0], wo_ref[A_Z:A_Z + 256, :])
    mix = mix + _mm(oc_ref[0], wo_ref[A_Z + 256:A_Z + 512, :])
    x = x_ref[0] + m[2:3] * mix
    h = _modulated(x, g_ref[...], m[3:4], m[4:5]).astype(BF16)
    fc = 1024
    acc = None
    for f in range(D_FF // fc):
        a = jnp.maximum(_mm(h, w1_ref[:, f * fc:(f + 1) * fc]), 0.0)
        y = _mm((a * a).astype(BF16), w2_ref[f * fc:(f + 1) * fc, :])
        acc = y if acc is None else acc + y
    o_ref[0] = x + m[5:6] * acc


def _out_mlp(xt, oa, ob, oc, mod, g, wo, w1, w2, n_ctx, need_ctx):
    bsz, tt, _ = xt.shape
    first = 0 if need_ctx else n_ctx // TOK_TILE
    nt = tt // TOK_TILE - first
    tile = lambda n: pl.BlockSpec((1, TOK_TILE, n), lambda b, t: (b, t + first, 0))
    att = pl.BlockSpec((1, TOK_TILE, 256), lambda b, t: (b, t, 0))
    full = lambda r, c: pl.BlockSpec((r, c), lambda b, t: (0, 0))
    return pl.pallas_call(
        _outmlp_kernel,
        grid=(bsz, nt),
        in_specs=[tile(D_MODEL), tile(A_Z), att, att,
                  pl.BlockSpec((1, N_MOD, D_MODEL), lambda b, t: (_mod_row(b, t + first), 0, 0)),
                  full(1, D_MODEL), full(D_MODEL, D_MODEL), full(D_MODEL, D_FF), full(D_FF, D_MODEL)],
        out_specs=pl.BlockSpec((1, TOK_TILE, D_MODEL), lambda b, t: (b, t, 0)),
        out_shape=jax.ShapeDtypeStruct((bsz, nt * TOK_TILE, D_MODEL), F32),
        compiler_params=_params(("parallel", "parallel")),
    )(xt, oa, ob, oc, mod, g, wo, w1, w2)


_HEAD_PERM = (0, 2, 1, 3)


def _perm_cols(n_heads_q):
    cols = []
    for h in _HEAD_PERM:
        cols.extend(range(h * HEAD_DIM, (h + 1) * HEAD_DIM))
    return cols


def _gate_cols():
    base_beta = A_QKV + A_Z
    base_alpha = base_beta + A_GATES
    cols = []
    for pair in range(GDN_HEADS // 2):
        for base in (base_beta, base_alpha):
            for d in range(2):
                for j in range(2):
                    cols.append(base + d * GDN_HEADS + 2 * pair + j)
    return cols


def _gate_param_lanes(p):
    out = jnp.zeros((LANES,), F32)
    idx, src = [], []
    for pair in range(GDN_HEADS // 2):
        for d in range(2):
            for j in range(2):
                idx.append(pair * 8 + 4 + d * 2 + j)
                src.append(d * GDN_HEADS + 2 * pair + j)
    return out.at[jnp.array(idx)].set(p.reshape(-1)[jnp.array(src)]).reshape(1, LANES)


def _rope_tables(n_ctx, n_lat):
    rows = n_lat // GRID_W
    row = jnp.repeat(jnp.arange(rows, dtype=F32), GRID_W)
    col = jnp.tile(jnp.arange(GRID_W, dtype=F32), rows)
    half = HEAD_DIM // 4
    inv_freq = ROPE_THETA ** (-jnp.arange(half, dtype=F32) / half)
    ang_r = row[:, None] * inv_freq
    ang_c = col[:, None] * inv_freq
    cr, sr, cc, sc = jnp.cos(ang_r), jnp.sin(ang_r), jnp.cos(ang_c), jnp.sin(ang_c)
    cos = jnp.concatenate([cr, cr, cc, cc], axis=-1)
    sin = jnp.concatenate([-sr, sr, -sc, sc], axis=-1)
    cos = jnp.concatenate([jnp.ones((n_ctx, HEAD_DIM), F32), cos], axis=0)
    sin = jnp.concatenate([jnp.zeros((n_ctx, HEAD_DIM), F32), sin], axis=0)
    return jnp.tile(cos, (1, 2)), jnp.tile(sin, (1, 2))


def _pad_rows(a, rows):
    return jnp.concatenate([a, jnp.zeros((rows - a.shape[0],) + a.shape[1:], a.dtype)], axis=0)


def kernel(x, c, ctx, c_ctx, w_mod, b_mod, g_attn, w_in, gdn_conv_w, gdn_a_log, gdn_dt_bias, gdn_norm_g,
           ga_q_norm_g, ga_k_norm_g, wa_q_norm_g, wa_k_norm_g, wa_sink, w_out, g_mlp, w_mlp_in, w_mlp_out):
    bsz, n_lat, _ = x.shape
    n_ctx = ctx.shape[1]
    depth = w_mod.shape[0]
    assert bsz <= 8 and n_ctx == TOK_TILE and n_lat % TOK_TILE == 0

    xt = jnp.concatenate([ctx, x], axis=1)
    cond16 = _pad_rows(jnp.concatenate([_pad_rows(c, 8), c_ctx[None, :]], axis=0), 16)
    cos_t, sin_t = _rope_tables(n_ctx, n_lat)

    b0 = A_QKV + A_Z + 2 * A_GATES
    n_pairs = GDN_HEADS // 2

    def head_perm(a, axis):
        blocks = [lax.slice_in_dim(a, h * HEAD_DIM, (h + 1) * HEAD_DIM, axis=axis) for h in _HEAD_PERM]
        return jnp.concatenate(blocks, axis=axis)

    for l in range(depth):
        need_ctx = l < depth - 1
        mod = _ada_mod(cond16, w_mod[l], b_mod[l]).reshape(16, N_MOD, D_MODEL)
        w = w_in[l]
        wg = w[:, A_QKV + A_Z:b0].reshape(D_MODEL, 2, 2, n_pairs, 2).transpose(0, 3, 1, 2, 4).reshape(D_MODEL, -1)
        parts = [w[:, :A_QKV + A_Z], wg, jnp.zeros((D_MODEL, LANES - 2 * A_GATES), F32)]
        for base in (b0, b0 + B_QKV):
            parts += [head_perm(w[:, base:base + 256], 1), w[:, base + 256:base + B_QKV]]
        w_in_l = jnp.concatenate(parts, axis=1).astype(BF16)
        gains = _pad_rows(jnp.stack([jnp.tile(g, 2) for g in
                                     (ga_q_norm_g[l], ga_k_norm_g[l], wa_q_norm_g[l], wa_k_norm_g[l])]), 8)
        a_qkv, a_z, beta, gcum, gtot, bq, bk, bv, cq, ck, cv = _in_proj(
            xt, mod, g_attn[l].reshape(1, D_MODEL), w_in_l, _pad_rows(gdn_conv_w[l], 8),
            _gate_param_lanes(gdn_a_log[l]), _gate_param_lanes(gdn_dt_bias[l]), cos_t, sin_t, gains, n_ctx)

        o_a = _gdn(a_qkv, beta, gcum, gtot, a_z, jnp.tile(gdn_norm_g[l], 2).reshape(1, LANES), n_ctx)

        o_b = _global_attn(bq, bk, bv, n_ctx, need_ctx)
        sink8 = _pad_rows(jnp.broadcast_to(wa_sink[l][:, None], (WA_HEADS, LANES)), 8)
        o_c = _window_attn(cq, ck, cv, sink8, n_ctx, need_ctx)

        wo = jnp.concatenate([w_out[l][:A_Z], head_perm(w_out[l][A_Z:A_Z + 256], 0),
                              head_perm(w_out[l][A_Z + 256:], 0)], axis=0).astype(BF16)
        xt = _out_mlp(xt, o_a, o_b, o_c, mod, g_mlp[l].reshape(1, D_MODEL), wo, w_mlp_in[l].astype(BF16),
                      w_mlp_out[l].astype(BF16), n_ctx, need_ctx)
    return xt
```

```python
import functools

import jax
import jax.numpy as jnp
from jax import lax
from jax.experimental import pallas as pl
from jax.experimental.pallas import tpu as pltpu

F32 = jnp.float32
BF16 = jnp.bfloat16

D_MODEL = 1024
GRID_W = 64
HEAD_DIM = 64
ROPE_THETA = 10000.0
GDN_HEADS = 8
GDN_CONV = 5
CHUNK = 64
GA_HEADS = 4
WA_HEADS = 4
WINDOW = 128
D_FF = 4 * D_MODEL
N_MOD = 6
RMS_EPS = 1e-6
A_QKV = 1536
A_Z = 512
A_GATES = 16
B_QKV = 512
C_QKV = 512
LANES = 128
TOK_TILE = 256
WQ_TILE = 256
PREP_UNROLL = 4
FIN_UNROLL = 4
PAR_GROUP = 9
NEG_BIG = -1e30
VMEM_LIMIT = 56 * 1024 * 1024


def _mm(a, b):
    return jnp.dot(a, b, preferred_element_type=F32)


def _nt(a, b):
    return lax.dot_general(a, b, (((1,), (1,)), ((), ())), preferred_element_type=F32)


def _sigmoid(x):
    return 1.0 / (1.0 + jnp.exp(-x))


def _params(sem):
    return pltpu.CompilerParams(dimension_semantics=sem, vmem_limit_bytes=VMEM_LIMIT)


def _mod_kernel(c_ref, w_ref, b_ref, o_ref):
    c = c_ref[...]
    s = c * _sigmoid(c)
    o_ref[...] = _mm(s.astype(BF16), w_ref[...].astype(BF16)) + b_ref[...]


def _ada_mod(cond16, w_mod, b_mod):
    n = N_MOD * D_MODEL
    tn = 1024
    return pl.pallas_call(
        _mod_kernel,
        grid=(n // tn,),
        in_specs=[pl.BlockSpec((16, D_MODEL), lambda j: (0, 0)),
                  pl.BlockSpec((D_MODEL, tn), lambda j: (0, j)),
                  pl.BlockSpec((1, tn), lambda j: (0, j))],
        out_specs=pl.BlockSpec((16, tn), lambda j: (0, j)),
        out_shape=jax.ShapeDtypeStruct((16, n), F32),
        compiler_params=_params(("arbitrary",)),
    )(cond16, w_mod, b_mod.reshape(1, n))


def _mod_row(b, t):
    return jnp.where(t == 0, 8, b)


def _modulated(x, g, shift, scale):
    ms = jnp.mean(x * x, axis=-1, keepdims=True)
    y = x * lax.rsqrt(ms + RMS_EPS) * g
    return y * (1.0 + scale) + shift


IN_COLS = A_QKV + A_Z + LANES + B_QKV + C_QKV


def _pair_sumsq_rsqrt(y, lo, eps, denom):
    yy = y * y
    s0 = jnp.sum(jnp.where(lo, yy, 0.0), axis=-1, keepdims=True)
    s1 = jnp.sum(jnp.where(lo, 0.0, yy), axis=-1, keepdims=True)
    return jnp.where(lo, lax.rsqrt(s0 * (1.0 / denom) + eps), lax.rsqrt(s1 * (1.0 / denom) + eps))


def _norm_rope(t, gain, cos, sin, lo, swap_lo):
    y = t * _pair_sumsq_rsqrt(t, lo, RMS_EPS, float(HEAD_DIM)) * gain
    sw = jnp.where(swap_lo, pltpu.roll(y, LANES - 16, 1), pltpu.roll(y, 16, 1))
    return y * cos + sw * sin


def _stream_specs(ctx_arr, lat_arr, first=0):
    lat_off = 1 if lat_arr is ctx_arr else 0
    return [pl.BlockSpec((1, TOK_TILE, D_MODEL), lambda b, t: (b, 0, 0)),
            pl.BlockSpec((1, TOK_TILE, D_MODEL), lambda b, t: (b, jnp.maximum(t + first - 1, 0) + lat_off, 0))]


def _stream_tile(xc_ref, xl_ref, t):
    return jnp.where(t == 0, xc_ref[0], xl_ref[0])


def _inproj_kernel(xc_ref, xl_ref, mod_ref, g_ref, w_ref, alog_ref, dtb_ref, cos_ref, sin_ref, gn_ref,
                   oa_ref, oz_ref, ob_ref, og_ref, ot_ref, bq_ref, bk_ref, bv_ref, cq_ref, ck_ref, cv_ref):
    m = mod_ref[0]
    x = _stream_tile(xc_ref, xl_ref, pl.program_id(1))
    h = _modulated(x, g_ref[...], m[0:1], m[1:2]).astype(BF16)
    lane = lax.broadcasted_iota(jnp.int32, (1, LANES), 1)
    lo = lane < HEAD_DIM
    oa_ref[0] = _mm(h, w_ref[:, 0:A_QKV])
    off = A_QKV
    oz_ref[0] = _mm(h, w_ref[:, off:off + A_Z])
    off += A_Z

    gt = _mm(h, w_ref[:, off:off + LANES])
    off += LANES
    a = gt + dtb_ref[...]
    sp = jnp.maximum(a, 0.0) + jnp.log(1.0 + jnp.exp(-jnp.abs(a)))
    gl = -jnp.exp(alog_ref[...]) * sp
    g_hi = gl.astype(BF16)
    r1 = gl - g_hi.astype(F32)
    g_mid = r1.astype(BF16)
    g_lo = (r1 - g_mid.astype(F32)).astype(BF16)
    ri = lax.broadcasted_iota(jnp.int32, (TOK_TILE, TOK_TILE), 0)
    ci = lax.broadcasted_iota(jnp.int32, (TOK_TILE, TOK_TILE), 1)
    same = (ri // CHUNK) == (ci // CHUNK)
    blk = jnp.where(same, 1.0, 0.0).astype(BF16)
    tri = jnp.where(jnp.logical_and(same, ri >= ci), 1.0, 0.0).astype(BF16)
    gpre = _mm(tri, g_hi) + _mm(tri, g_mid) + _mm(tri, g_lo)
    tot = _mm(blk, g_hi) + _mm(blk, g_mid) + _mm(blk, g_lo)
    ob_ref[0] = _sigmoid(gt)
    og_ref[0] = jnp.where((lane % 4) >= 2, tot - gpre + gl, gpre)
    ot_ref[0] = tot

    swap_lo = (lane % 32) < 16
    cos = cos_ref[...]
    sin = sin_ref[...]
    gn = gn_ref[...]
    scale = HEAD_DIM ** -0.5
    for q_ref, k_ref, v_ref, gi in ((bq_ref, bk_ref, bv_ref, 0), (cq_ref, ck_ref, cv_ref, 2)):
        p = _mm(h, w_ref[:, off:off + B_QKV])
        off += B_QKV
        for j in range(2):
            t = p[:, j * LANES:(j + 1) * LANES]
            q_ref[0, :, j * LANES:(j + 1) * LANES] = (
                _norm_rope(t, gn[gi:gi + 1], cos, sin, lo, swap_lo) * scale).astype(BF16)
        k_ref[0] = _norm_rope(p[:, 2 * LANES:3 * LANES], gn[gi + 1:gi + 2], cos, sin, lo, swap_lo).astype(BF16)
        v_ref[0] = p[:, 3 * LANES:].astype(BF16)


def _in_proj(ctx_arr, lat_arr, mod, g, w, alog_l, dtb_l, cos_t, sin_t, gains):
    bsz = ctx_arr.shape[0]
    tt = cos_t.shape[0]
    nt = tt // TOK_TILE
    tile = lambda n: pl.BlockSpec((1, TOK_TILE, n), lambda b, t: (b, t, 0))
    tab = pl.BlockSpec((TOK_TILE, LANES), lambda b, t: (t, 0))
    vec = pl.BlockSpec((1, LANES), lambda b, t: (0, 0))
    f32 = lambda n: jax.ShapeDtypeStruct((bsz, tt, n), F32)
    b16 = lambda n: jax.ShapeDtypeStruct((bsz, tt, n), BF16)
    return pl.pallas_call(
        _inproj_kernel,
        grid=(bsz, nt),
        in_specs=_stream_specs(ctx_arr, lat_arr)
                 + [pl.BlockSpec((1, N_MOD, D_MODEL), lambda b, t: (_mod_row(b, t), 0, 0)),
                    pl.BlockSpec((1, D_MODEL), lambda b, t: (0, 0)),
                    pl.BlockSpec((D_MODEL, IN_COLS), lambda b, t: (0, 0)), vec, vec,
                    tab, tab, pl.BlockSpec((8, LANES), lambda b, t: (0, 0))],
        out_specs=[tile(A_QKV), tile(A_Z), tile(LANES), tile(LANES), tile(LANES)]
                  + [tile(256), tile(LANES), tile(LANES)] * 2,
        out_shape=[f32(A_QKV), f32(A_Z), f32(LANES), f32(LANES), f32(LANES)]
                  + [b16(256), b16(LANES), b16(LANES)] * 2,
        compiler_params=_params(("parallel", "arbitrary")),
    )(ctx_arr, lat_arr, mod, g, w, alog_l, dtb_l, cos_t, sin_t, gains)


def _gdn_kernel(q_ref, k_ref, v_ref, beta_ref, gc_ref, tot_ref, z_ref, cwq_ref, cwk_ref, cwv_ref, ng_ref, o_ref,
                qs, ks, vs, bs, gs, ts, grs, oacc, mq, bns, ext, *, n_ctx_chunks, n_chunks):
    hp = pl.program_id(1)
    tt = n_chunks * CHUNK
    lane = lax.broadcasted_iota(jnp.int32, (1, LANES), 1)
    lo = lane < HEAD_DIM
    shift = (LANES - 8 * hp) % LANES

    def prep_body(i, carry):
        for u in range(PREP_UNROLL):
            c = i * PREP_UNROLL + u
            r0 = pl.multiple_of(c * CHUNK, CHUNK)
            c8 = pl.multiple_of(c * 8, 8)
            first = jnp.logical_or(c == 0, c == n_ctx_chunks)
            last = jnp.logical_or(c == n_ctx_chunks - 1, c == n_chunks - 1)
            for i3, (src, cw_ref, dst, norm) in enumerate(((q_ref, cwq_ref, qs, True),
                                                           (k_ref, cwk_ref, ks, True),
                                                           (v_ref, cwv_ref, vs, False))):
                e = 3 * u + i3
                prev = src[0, pl.ds(jnp.maximum(r0 - 8, 0), 8), :]
                nxt = src[0, pl.ds(jnp.minimum(r0 + CHUNK, tt - 8), 8), :]
                ext[e, 0:8, :] = jnp.where(first, 0.0, prev)
                ext[e, 8:8 + CHUNK, :] = src[0, pl.ds(r0, CHUNK), :]
                ext[e, 8 + CHUNK:16 + CHUNK, :] = jnp.where(last, 0.0, nxt)
                cw = cw_ref[...]
                y = ext[e, 6:6 + CHUNK, :] * cw[0:1]
                for tap in range(1, GDN_CONV):
                    y = y + ext[e, 6 + tap:6 + tap + CHUNK, :] * cw[tap:tap + 1]
                y = y * _sigmoid(y)
                if norm:
                    y = y * _pair_sumsq_rsqrt(y, lo, 1e-6, 1.0)
                dst[pl.ds(r0, CHUNK), :] = y
            bs[pl.ds(r0, CHUNK), :] = pltpu.roll(beta_ref[0, pl.ds(r0, CHUNK), :], shift, 1)
            gcr = pltpu.roll(gc_ref[0, pl.ds(r0, CHUNK), :], shift, 1)
            gs[pl.ds(r0, CHUNK), :] = gcr
            grs[pl.ds(c8, 8), :] = jnp.concatenate([gcr, gcr], axis=0).T[0:8, :]
            ts[pl.ds(c8, 8), :] = pltpu.roll(tot_ref[0, pl.ds(r0, 8), :], shift, 1)
        return carry

    lax.fori_loop(0, n_chunks // PREP_UNROLL, prep_body, 0)

    ri = lax.broadcasted_iota(jnp.int32, (LANES, LANES), 0)
    ci = lax.broadcasted_iota(jnp.int32, (LANES, LANES), 1)
    bd = (ri < HEAD_DIM) == (ci < HEAD_DIM)
    ti = lax.broadcasted_iota(jnp.int32, (CHUNK, LANES), 0)
    tj = lax.broadcasted_iota(jnp.int32, (CHUNK, LANES), 1) % CHUNK
    blk8 = (ti // 8) == (tj // 8)
    merge_masks = [jnp.logical_and((ti // (2 * m)) == (tj // (2 * m)), (ti // m) != (tj // m))
                   for m in (8, 16, 32)]
    masks = [((ti >= tj), (ti > tj)), ((ti <= tj), (ti < tj))]

    def blockdiag(x):
        xb = x.astype(BF16)
        return jnp.where(bd, jnp.concatenate([xb, xb], axis=0), jnp.zeros((LANES, LANES), BF16))

    def chunk_par(c, d):
        incl, strict = masks[d]
        r0 = pl.multiple_of(c * CHUNK, CHUNK)
        ib, ig = 2 * d, 4 + 2 * d
        bt = bs[pl.ds(r0, CHUNK), :]
        gt = gs[pl.ds(r0, CHUNK), :]
        bcol = jnp.where(lo, bt[:, ib:ib + 1], bt[:, ib + 1:ib + 2])
        gcol = jnp.where(lo, gt[:, ig:ig + 1], gt[:, ig + 1:ig + 2])
        k = ks[pl.ds(r0, CHUNK), :]
        q = qs[pl.ds(r0, CHUNK), :] * (HEAD_DIM ** -0.5)
        x = _nt(jnp.concatenate([k * bcol, q], axis=0).astype(BF16), blockdiag(k))
        yield
        gr8 = grs[pl.ds(pl.multiple_of(c * 8, 8), 8), :]
        grow = jnp.where(lo, gr8[ig:ig + 1, :], gr8[ig + 1:ig + 2, :])
        dec = jnp.exp(jnp.where(incl, gcol - grow, NEG_BIG))
        low = jnp.where(strict, x[:CHUNK] * dec, 0.0)
        intra = (x[CHUNK:] * dec).astype(BF16)
        l0 = jnp.where(blk8, low, 0.0)
        p = _mm(l0.astype(BF16), blockdiag(l0))
        yield
        n = -l0
        y = _mm(jnp.concatenate([n, p], axis=0).astype(BF16), blockdiag(p))
        yield
        n = n + p + y[:CHUNK]
        p = y[CHUNK:]
        y = _mm(n.astype(BF16), blockdiag(p))
        yield
        n = n + p + y
        for cmask in merge_masks:
            cm = jnp.where(cmask, low, 0.0)
            y = _mm(n.astype(BF16), blockdiag(cm))
            yield
            tc = cm + y
            y = _mm(tc.astype(BF16), blockdiag(n))
            yield
            n = n - tc - y
        vb = vs[pl.ds(r0, CHUNK), :] * bcol
        kbg = k * bcol * jnp.exp(gcol)
        y = _mm(n.astype(BF16), jnp.concatenate([blockdiag(vb), blockdiag(kbg)], axis=1))
        yield
        u = vb + y[:, :LANES]
        w = kbg + y[:, LANES:]
        bu, bw = blockdiag(u), blockdiag(w)
        iuw = _mm(intra, jnp.concatenate([bu, bw], axis=1))
        tl8 = ts[pl.ds(pl.multiple_of(c * 8, 8), 8), :]
        tl = jnp.where(lo, tl8[0:1, ig:ig + 1], tl8[0:1, ig + 1:ig + 2])
        ke = k * jnp.exp(tl - gcol)
        kett = jnp.concatenate([ke, ke], axis=0).T
        ket = jnp.where(lo, kett[:CHUNK], kett[CHUNK:]).astype(BF16)
        mb = _mm(ket, jnp.concatenate([bw, bu], axis=1))
        yield
        qe = q * jnp.exp(gcol) - iuw[:, LANES:]
        return jnp.concatenate([qe, mb[:, :LANES]], axis=0).astype(BF16), mb[:, LANES:], iuw[:, :LANES]

    def interleave(gens):
        res = [None] * len(gens)
        live = list(range(len(gens)))
        while live:
            for idx in list(live):
                try:
                    next(gens[idx])
                except StopIteration as stop:
                    res[idx] = stop.value
                    live.remove(idx)
        return res

    def scan_chunk(p, d):
        if d == 0:
            return p
        return jnp.where(p < n_ctx_chunks, n_ctx_chunks - 1 - p, n_chunks + n_ctx_chunks - 1 - p)

    def seq_chain(g, d, s):
        ig = 4 + 2 * d
        for j in range(PAR_GROUP):
            c = scan_chunk(g * PAR_GROUP + j, d)
            r0 = pl.multiple_of(c * CHUNK, CHUNK)
            tl8 = ts[pl.ds(pl.multiple_of(c * 8, 8), 8), :]
            arow = jnp.exp(jnp.where(lo, tl8[0:1, ig:ig + 1], tl8[0:1, ig + 1:ig + 2]))
            y = _mm(mq[d, c], blockdiag(s))
            yield
            oacc[d, pl.ds(r0, CHUNK), :] = oacc[d, pl.ds(r0, CHUNK), :] + y[:CHUNK]
            s = arow * s - y[CHUNK:] + bns[d, c]
        return s

    def group_step(g, states, with_par):
        gens, cs = [], []
        if with_par:
            cs = [(scan_chunk(g * PAR_GROUP + j, d), d) for j in range(PAR_GROUP) for d in range(2)]
            gens = [chunk_par(c, d) for c, d in cs]
        if states is not None:
            gens += [seq_chain(g - 1, d, states[d]) for d in range(2)]
        res = interleave(gens)
        for (c, d), (qm, bn, iu) in zip(cs, res):
            mq[d, c] = qm
            bns[d, c] = bn
            oacc[d, pl.ds(pl.multiple_of(c * CHUNK, CHUNK), CHUNK), :] = iu
        return None if states is None else (res[-2], res[-1])

    n_groups = n_chunks // PAR_GROUP
    zero = jnp.zeros((CHUNK, LANES), F32)
    group_step(0, None, True)
    states = lax.fori_loop(1, n_groups, lambda g, st: group_step(g, st, True), (zero, zero))
    group_step(n_groups, states, False)

    ng = ng_ref[...]

    def fin(i, carry):
        for u in range(FIN_UNROLL):
            r0 = pl.multiple_of((i * FIN_UNROLL + u) * CHUNK, CHUNK)
            o = oacc[0, pl.ds(r0, CHUNK), :] + oacc[1, pl.ds(r0, CHUNK), :]
            z = z_ref[0, pl.ds(r0, CHUNK), :]
            y = o * _pair_sumsq_rsqrt(o, lo, RMS_EPS, float(HEAD_DIM)) * ng
            o_ref[0, pl.ds(r0, CHUNK), :] = (y * (z * _sigmoid(z))).astype(o_ref.dtype)
        return carry

    lax.fori_loop(0, n_chunks // FIN_UNROLL, fin, 0)


def _gdn(a_qkv, beta, gcum, gtot, a_z, cw8, ng2, n_ctx):
    bsz, tt, _ = a_qkv.shape
    n_pairs = GDN_HEADS // 2
    tok = lambda off: pl.BlockSpec((1, tt, LANES), lambda b, h: (b, 0, h + off))
    par = lambda off: pl.BlockSpec((8, LANES), lambda b, h: (0, h + off))
    gate = pl.BlockSpec((1, tt, LANES), lambda b, h: (b, 0, 0))
    kern = functools.partial(_gdn_kernel, n_ctx_chunks=n_ctx // CHUNK, n_chunks=tt // CHUNK)
    return pl.pallas_call(
        kern,
        grid=(bsz, n_pairs),
        in_specs=[tok(0), tok(n_pairs), tok(2 * n_pairs), gate, gate, gate, tok(0),
                  par(0), par(n_pairs), par(2 * n_pairs), pl.BlockSpec((1, LANES), lambda b, h: (0, 0))],
        out_specs=tok(0),
        out_shape=jax.ShapeDtypeStruct((bsz, tt, A_Z), BF16),
        scratch_shapes=[pltpu.VMEM((tt, LANES), F32)] * 5
                       + [pltpu.VMEM((tt // CHUNK * 8, LANES), F32)] * 2
                       + [pltpu.VMEM((2, tt, LANES), F32),
                          pltpu.VMEM((2, tt // CHUNK, 2 * CHUNK, LANES), BF16),
                          pltpu.VMEM((2, tt // CHUNK, CHUNK, LANES), F32),
                          pltpu.VMEM((3 * PREP_UNROLL, CHUNK + 16, LANES), F32)],
        compiler_params=_params(("parallel", "parallel")),
    )(a_qkv, a_qkv, a_qkv, beta, gcum, gtot, a_z, cw8, cw8, cw8, ng2)


VT_STEP = 256


def _head_queries(q_ref, lo):
    qms = []
    for tile in range(2):
        qt = q_ref[0, :, tile * LANES:(tile + 1) * LANES]
        for hh in range(2):
            qms.append(jnp.where(lo if hh == 0 else jnp.logical_not(lo), qt, jnp.zeros_like(qt)))
    return qms


def _fill_vt(v_ref, vt):
    for r in range(0, v_ref.shape[1], VT_STEP):
        vt[:, r:r + VT_STEP] = v_ref[0, r:r + VT_STEP, :].astype(F32).T.astype(BF16)


def _softmax_values(scores_t, vv_t, extra=None):
    probs, rinv = [], []
    for i, st in enumerate(scores_t):
        mx = jnp.max(st, axis=0, keepdims=True)
        if extra is not None:
            mx = jnp.maximum(mx, extra[i])
        p = jnp.exp(st - mx)
        l = jnp.sum(p, axis=0, keepdims=True)
        if extra is not None:
            l = l + jnp.exp(extra[i] - mx)
        rinv.append(1.0 / l)
        probs.append(p.astype(BF16))
    return [(_mm(vv_t, p) * r).T for p, r in zip(probs, rinv)]


def _store_heads(o_ref, outs, lo):
    for tile in range(2):
        o_ref[0, :, tile * LANES:(tile + 1) * LANES] = jnp.where(
            lo, outs[2 * tile], outs[2 * tile + 1]).astype(o_ref.dtype)


def _gattn_kernel(q_ref, k_ref, v_ref, o_ref, vt, *, first_tile, n_ctx):
    t = pl.program_id(1) + first_tile
    lane = lax.broadcasted_iota(jnp.int32, (1, LANES), 1)
    lo = lane < HEAD_DIM

    @pl.when(pl.program_id(1) == 0)
    def _():
        _fill_vt(v_ref, vt)

    def attend(n_keys):
        kk = k_ref[0, 0:n_keys, :]
        scores_t = [_nt(kk, qm) for qm in _head_queries(q_ref, lo)]
        _store_heads(o_ref, _softmax_values(scores_t, vt[:, 0:n_keys]), lo)

    if first_tile == 0:
        @pl.when(t == 0)
        def _():
            attend(n_ctx)

    @pl.when(t > 0)
    def _():
        attend(k_ref.shape[1])


def _global_attn(q, k, v, n_ctx, need_ctx):
    bsz, tt, _ = q.shape
    first = 0 if need_ctx else n_ctx // TOK_TILE
    nt = tt // TOK_TILE - first
    kv = pl.BlockSpec((1, tt, LANES), lambda b, t: (b, 0, 0))
    return pl.pallas_call(
        functools.partial(_gattn_kernel, first_tile=first, n_ctx=n_ctx),
        grid=(bsz, nt),
        in_specs=[pl.BlockSpec((1, TOK_TILE, 256), lambda b, t: (b, t + first, 0)), kv, kv],
        out_specs=pl.BlockSpec((1, TOK_TILE, 256), lambda b, t: (b, t, 0)),
        out_shape=jax.ShapeDtypeStruct((bsz, nt * TOK_TILE, 256), BF16),
        scratch_shapes=[pltpu.VMEM((LANES, tt), BF16)],
        compiler_params=_params(("arbitrary", "arbitrary")),
    )(q, k, v)


def _wattn_kernel(q_ref, k_ref, v_ref, sink_ref, o_ref, vt, *, first_tile, n_ctx, n_lat):
    t = pl.program_id(1) + first_tile
    n_ctx_tiles = n_ctx // WQ_TILE
    lane = lax.broadcasted_iota(jnp.int32, (1, LANES), 1)
    lo = lane < HEAD_DIM
    kw = WQ_TILE + 2 * WINDOW
    sink = sink_ref[...]
    sinks = [sink[2 * hh + tile:2 * hh + tile + 1, 0:1] for tile in range(2) for hh in range(2)]

    @pl.when(pl.program_id(1) == 0)
    def _():
        _fill_vt(v_ref, vt)

    def attend(win):
        qms = _head_queries(q_ref, lo)
        if win:
            n = t - n_ctx_tiles
            ws = jnp.clip(n * WQ_TILE - WINDOW, 0, n_lat - kw)
            start = pl.multiple_of(n_ctx + ws, WINDOW)
            kk = jnp.concatenate([k_ref[0, 0:n_ctx, :], k_ref[0, pl.ds(start, kw), :]], axis=0)
            vv = jnp.concatenate([vt[:, 0:n_ctx], vt[:, pl.ds(start, kw)]], axis=1)
            row = lax.broadcasted_iota(jnp.int32, (n_ctx + kw, WQ_TILE), 0)
            qpos = n * WQ_TILE + lax.broadcasted_iota(jnp.int32, (n_ctx + kw, WQ_TILE), 1)
            valid = jnp.logical_or(row < n_ctx, jnp.abs(ws + row - n_ctx - qpos) <= WINDOW)
            scores_t = [jnp.where(valid, _nt(kk, qm), NEG_BIG) for qm in qms]
        else:
            kk = k_ref[0, 0:n_ctx, :]
            vv = vt[:, 0:n_ctx]
            scores_t = [_nt(kk, qm) for qm in qms]
        _store_heads(o_ref, _softmax_values(scores_t, vv, sinks), lo)

    if first_tile == 0:
        @pl.when(t < n_ctx_tiles)
        def _():
            attend(False)

    @pl.when(t >= n_ctx_tiles)
    def _():
        attend(True)


def _window_attn(q, k, v, sink8, n_ctx, need_ctx):
    bsz, tt, _ = q.shape
    first = 0 if need_ctx else n_ctx // WQ_TILE
    nt = tt // WQ_TILE - first
    kv = pl.BlockSpec((1, tt, LANES), lambda b, t: (b, 0, 0))
    return pl.pallas_call(
        functools.partial(_wattn_kernel, first_tile=first, n_ctx=n_ctx, n_lat=tt - n_ctx),
        grid=(bsz, nt),
        in_specs=[pl.BlockSpec((1, WQ_TILE, 256), lambda b, t: (b, t + first, 0)), kv, kv,
                  pl.BlockSpec((8, LANES), lambda b, t: (0, 0))],
        out_specs=pl.BlockSpec((1, WQ_TILE, 256), lambda b, t: (b, t, 0)),
        out_shape=jax.ShapeDtypeStruct((bsz, nt * WQ_TILE, 256), BF16),
        scratch_shapes=[pltpu.VMEM((LANES, tt), BF16)],
        compiler_params=_params(("arbitrary", "arbitrary")),
    )(q, k, v, sink8)


def _outmlp_kernel(xc_ref, xl_ref, oa_ref, ob_ref, oc_ref, mod_ref, g_ref, wo_ref, w1_ref, w2_ref, o_ref, *,
                   first_tile):
    m = mod_ref[0]
    mix = _mm(oa_ref[0], wo_ref[0:A_Z, :])
    mix = mix + _mm(ob_ref[0], wo_ref[A_Z:A_Z + 256, :])
    mix = mix + _mm(oc_ref[0], wo_ref[A_Z + 256:A_Z + 512, :])
    x = _stream_tile(xc_ref, xl_ref, pl.program_id(1) + first_tile) + m[2:3] * mix
    h = _modulated(x, g_ref[...], m[3:4], m[4:5]).astype(BF16)
    fc = 1024
    acc = None
    for f in range(D_FF // fc):
        a = jnp.maximum(_mm(h, w1_ref[:, f * fc:(f + 1) * fc]), 0.0)
        y = _mm((a * a).astype(BF16), w2_ref[f * fc:(f + 1) * fc, :])
        acc = y if acc is None else acc + y
    o_ref[0] = x + m[5:6] * acc


def _out_mlp(ctx_arr, lat_arr, oa, ob, oc, mod, g, wo, w1, w2, n_ctx, need_ctx):
    bsz, tt, _ = oa.shape
    first = 0 if need_ctx else n_ctx // TOK_TILE
    nt = tt // TOK_TILE - first
    att = pl.BlockSpec((1, TOK_TILE, 256), lambda b, t: (b, t, 0))
    full = lambda r, c: pl.BlockSpec((r, c), lambda b, t: (0, 0))
    return pl.pallas_call(
        functools.partial(_outmlp_kernel, first_tile=first),
        grid=(bsz, nt),
        in_specs=_stream_specs(ctx_arr, lat_arr, first)
                 + [pl.BlockSpec((1, TOK_TILE, A_Z), lambda b, t: (b, t + first, 0)), att, att,
                    pl.BlockSpec((1, N_MOD, D_MODEL), lambda b, t: (_mod_row(b, t + first), 0, 0)),
                    full(1, D_MODEL), full(D_MODEL, D_MODEL), full(D_MODEL, D_FF), full(D_FF, D_MODEL)],
        out_specs=pl.BlockSpec((1, TOK_TILE, D_MODEL), lambda b, t: (b, t, 0)),
        out_shape=jax.ShapeDtypeStruct((bsz, nt * TOK_TILE, D_MODEL), F32),
        compiler_params=_params(("parallel", "arbitrary")),
    )(ctx_arr, lat_arr, oa, ob, oc, mod, g, wo, w1, w2)


_HEAD_PERM = (0, 2, 1, 3)


def _gate_param_lanes(p):
    out = jnp.zeros((LANES,), F32)
    idx, src = [], []
    for pair in range(GDN_HEADS // 2):
        for d in range(2):
            for j in range(2):
                idx.append(pair * 8 + 4 + d * 2 + j)
                src.append(d * GDN_HEADS + 2 * pair + j)
    return out.at[jnp.array(idx)].set(p.reshape(-1)[jnp.array(src)]).reshape(1, LANES)


def _rope_tables(n_ctx, n_lat):
    rows = n_lat // GRID_W
    row = jnp.repeat(jnp.arange(rows, dtype=F32), GRID_W)
    col = jnp.tile(jnp.arange(GRID_W, dtype=F32), rows)
    half = HEAD_DIM // 4
    inv_freq = ROPE_THETA ** (-jnp.arange(half, dtype=F32) / half)
    ang_r = row[:, None] * inv_freq
    ang_c = col[:, None] * inv_freq
    cr, sr, cc, sc = jnp.cos(ang_r), jnp.sin(ang_r), jnp.cos(ang_c), jnp.sin(ang_c)
    cos = jnp.concatenate([cr, cr, cc, cc], axis=-1)
    sin = jnp.concatenate([-sr, sr, -sc, sc], axis=-1)
    cos = jnp.concatenate([jnp.ones((n_ctx, HEAD_DIM), F32), cos], axis=0)
    sin = jnp.concatenate([jnp.zeros((n_ctx, HEAD_DIM), F32), sin], axis=0)
    return jnp.tile(cos, (1, 2)), jnp.tile(sin, (1, 2))


def _pad_rows(a, rows):
    return jnp.concatenate([a, jnp.zeros((rows - a.shape[0],) + a.shape[1:], a.dtype)], axis=0)


def kernel(x, c, ctx, c_ctx, w_mod, b_mod, g_attn, w_in, gdn_conv_w, gdn_a_log, gdn_dt_bias, gdn_norm_g,
           ga_q_norm_g, ga_k_norm_g, wa_q_norm_g, wa_k_norm_g, wa_sink, w_out, g_mlp, w_mlp_in, w_mlp_out):
    bsz, n_lat, _ = x.shape
    n_ctx = ctx.shape[1]
    depth = w_mod.shape[0]
    assert bsz <= 8 and n_ctx == TOK_TILE and n_lat % TOK_TILE == 0

    ctx_arr, lat_arr = ctx, x
    cond16 = _pad_rows(jnp.concatenate([_pad_rows(c, 8), c_ctx[None, :]], axis=0), 16)
    cos_t, sin_t = _rope_tables(n_ctx, n_lat)

    b0 = A_QKV + A_Z + 2 * A_GATES
    n_pairs = GDN_HEADS // 2

    def head_perm(a, axis):
        blocks = [lax.slice_in_dim(a, h * HEAD_DIM, (h + 1) * HEAD_DIM, axis=axis) for h in _HEAD_PERM]
        return jnp.concatenate(blocks, axis=axis)

    for l in range(depth):
        need_ctx = l < depth - 1
        mod = _ada_mod(cond16, w_mod[l], b_mod[l]).reshape(16, N_MOD, D_MODEL)
        w = w_in[l]
        wg = w[:, A_QKV + A_Z:b0].reshape(D_MODEL, 2, 2, n_pairs, 2).transpose(0, 3, 1, 2, 4).reshape(D_MODEL, -1)
        parts = [w[:, :A_QKV + A_Z], wg, jnp.zeros((D_MODEL, LANES - 2 * A_GATES), F32)]
        for base in (b0, b0 + B_QKV):
            parts += [head_perm(w[:, base:base + 256], 1), w[:, base + 256:base + B_QKV]]
        w_in_l = jnp.concatenate(parts, axis=1).astype(BF16)
        gains = _pad_rows(jnp.stack([jnp.tile(g, 2) for g in
                                     (ga_q_norm_g[l], ga_k_norm_g[l], wa_q_norm_g[l], wa_k_norm_g[l])]), 8)
        a_qkv, a_z, beta, gcum, gtot, bq, bk, bv, cq, ck, cv = _in_proj(
            ctx_arr, lat_arr, mod, g_attn[l].reshape(1, D_MODEL), w_in_l, _gate_param_lanes(gdn_a_log[l]),
            _gate_param_lanes(gdn_dt_bias[l]), cos_t, sin_t, gains)

        o_a = _gdn(a_qkv, beta, gcum, gtot, a_z, _pad_rows(gdn_conv_w[l], 8),
                   jnp.tile(gdn_norm_g[l], 2).reshape(1, LANES), n_ctx)

        o_b = _global_attn(bq, bk, bv, n_ctx, need_ctx)
        sink8 = _pad_rows(jnp.broadcast_to(wa_sink[l][:, None], (WA_HEADS, LANES)), 8)
        o_c = _window_attn(cq, ck, cv, sink8, n_ctx, need_ctx)

        wo = jnp.concatenate([w_out[l][:A_Z], head_perm(w_out[l][A_Z:A_Z + 256], 0),
                              head_perm(w_out[l][A_Z + 256:], 0)], axis=0).astype(BF16)
        xt = _out_mlp(ctx_arr, lat_arr, o_a, o_b, o_c, mod, g_mlp[l].reshape(1, D_MODEL), wo,
                      w_mlp_in[l].astype(BF16), w_mlp_out[l].astype(BF16), n_ctx, need_ctx)
        ctx_arr = lat_arr = xt
    return xt
```

```python
import functools

import jax
import jax.numpy as jnp
from jax import lax
from jax.experimental import pallas as pl
from jax.experimental.pallas import tpu as pltpu

F32 = jnp.float32
BF16 = jnp.bfloat16

D_MODEL = 1024
GRID_W = 64
HEAD_DIM = 64
ROPE_THETA = 10000.0
GDN_HEADS = 8
GDN_CONV = 5
CHUNK = 64
GA_HEADS = 4
WA_HEADS = 4
WINDOW = 128
D_FF = 4 * D_MODEL
N_MOD = 6
RMS_EPS = 1e-6
A_QKV = 1536
A_Z = 512
A_GATES = 16
B_QKV = 512
C_QKV = 512
LANES = 128
TOK_TILE = 256
WQ_TILE = 256
PREP_UNROLL = 4
FIN_UNROLL = 9
PAR_GROUP = 9
NEG_BIG = -1e30
LOG2E = 1.4426950408889634
VMEM_LIMIT = 56 * 1024 * 1024


def _mm(a, b):
    return jnp.dot(a, b, preferred_element_type=F32)


def _nt(a, b):
    return lax.dot_general(a, b, (((1,), (1,)), ((), ())), preferred_element_type=F32)


def _sigmoid(x):
    return 1.0 / (1.0 + jnp.exp(-x))


def _params(sem):
    return pltpu.CompilerParams(dimension_semantics=sem, vmem_limit_bytes=VMEM_LIMIT)


def _mod_kernel(c_ref, w_ref, b_ref, o_ref):
    c = c_ref[...]
    s = c * _sigmoid(c)
    o_ref[...] = _mm(s.astype(BF16), w_ref[...].astype(BF16)) + b_ref[...]


def _ada_mod(cond16, w_mod, b_mod):
    n = N_MOD * D_MODEL
    tn = 1024
    return pl.pallas_call(
        _mod_kernel,
        grid=(n // tn,),
        in_specs=[pl.BlockSpec((16, D_MODEL), lambda j: (0, 0)),
                  pl.BlockSpec((D_MODEL, tn), lambda j: (0, j)),
                  pl.BlockSpec((1, tn), lambda j: (0, j))],
        out_specs=pl.BlockSpec((16, tn), lambda j: (0, j)),
        out_shape=jax.ShapeDtypeStruct((16, n), F32),
        compiler_params=_params(("arbitrary",)),
    )(cond16, w_mod, b_mod.reshape(1, n))


def _mod_row(b, t):
    return jnp.where(t == 0, 8, b)


def _modulated(x, g, shift, scale):
    ms = jnp.mean(x * x, axis=-1, keepdims=True)
    y = x * lax.rsqrt(ms + RMS_EPS) * g
    return y * (1.0 + scale) + shift


IN_COLS = A_QKV + A_Z + LANES + B_QKV + C_QKV


def _pair_sumsq_rsqrt(y, lo, eps, denom):
    yy = y * y
    s0 = jnp.sum(jnp.where(lo, yy, 0.0), axis=-1, keepdims=True)
    s1 = jnp.sum(jnp.where(lo, 0.0, yy), axis=-1, keepdims=True)
    return jnp.where(lo, lax.rsqrt(s0 * (1.0 / denom) + eps), lax.rsqrt(s1 * (1.0 / denom) + eps))


def _pair_sumsq_rsqrt_mxu(y, ones_bd, eps, denom):
    yy = y * y
    hi = yy.astype(BF16)
    lo_part = (yy - hi.astype(F32)).astype(BF16)
    ss = _mm(hi, ones_bd) + _mm(lo_part, ones_bd)
    return lax.rsqrt(ss * (1.0 / denom) + eps)


def _norm_rope(t, gain, cos, sin, lo, swap_lo):
    y = t * _pair_sumsq_rsqrt(t, lo, RMS_EPS, float(HEAD_DIM)) * gain
    sw = jnp.where(swap_lo, pltpu.roll(y, LANES - 16, 1), pltpu.roll(y, 16, 1))
    return y * cos + sw * sin


def _stream_specs(ctx_arr, lat_arr, first=0):
    lat_off = 1 if lat_arr is ctx_arr else 0
    return [pl.BlockSpec((1, TOK_TILE, D_MODEL), lambda b, t: (b, 0, 0)),
            pl.BlockSpec((1, TOK_TILE, D_MODEL), lambda b, t: (b, jnp.maximum(t + first - 1, 0) + lat_off, 0))]


def _stream_tile(xc_ref, xl_ref, t):
    return jnp.where(t == 0, xc_ref[0], xl_ref[0])


def _inproj_kernel(xc_ref, xl_ref, mod_ref, g_ref, w_ref, alog_ref, dtb_ref, cos_ref, sin_ref, gn_ref,
                   oa_ref, oz_ref, ob_ref, og_ref, ot_ref, bq_ref, bk_ref, bv_ref, cq_ref, ck_ref, cv_ref):
    m = mod_ref[0]
    x = _stream_tile(xc_ref, xl_ref, pl.program_id(1))
    h = _modulated(x, g_ref[...], m[0:1], m[1:2]).astype(BF16)
    lane = lax.broadcasted_iota(jnp.int32, (1, LANES), 1)
    lo = lane < HEAD_DIM
    oa_ref[0] = _mm(h, w_ref[:, 0:A_QKV])
    off = A_QKV
    oz_ref[0] = _mm(h, w_ref[:, off:off + A_Z])
    off += A_Z

    gt = _mm(h, w_ref[:, off:off + LANES])
    off += LANES
    a = gt + dtb_ref[...]
    sp = jnp.maximum(a, 0.0) + jnp.log(1.0 + jnp.exp(-jnp.abs(a)))
    gl = -jnp.exp(alog_ref[...]) * sp
    g_hi = gl.astype(BF16)
    r1 = gl - g_hi.astype(F32)
    g_mid = r1.astype(BF16)
    g_lo = (r1 - g_mid.astype(F32)).astype(BF16)
    ri = lax.broadcasted_iota(jnp.int32, (TOK_TILE, TOK_TILE), 0)
    ci = lax.broadcasted_iota(jnp.int32, (TOK_TILE, TOK_TILE), 1)
    same = (ri // CHUNK) == (ci // CHUNK)
    blk = jnp.where(same, 1.0, 0.0).astype(BF16)
    tri = jnp.where(jnp.logical_and(same, ri >= ci), 1.0, 0.0).astype(BF16)
    gpre = _mm(tri, g_hi) + _mm(tri, g_mid) + _mm(tri, g_lo)
    tot = _mm(blk, g_hi) + _mm(blk, g_mid) + _mm(blk, g_lo)
    ob_ref[0] = _sigmoid(gt)
    og_ref[0] = jnp.where((lane % 4) >= 2, tot - gpre + gl, gpre)
    ot_ref[0] = tot

    swap_lo = (lane % 32) < 16
    cos = cos_ref[...]
    sin = sin_ref[...]
    gn = gn_ref[...]
    scale = HEAD_DIM ** -0.5 * LOG2E
    for q_ref, k_ref, v_ref, gi in ((bq_ref, bk_ref, bv_ref, 0), (cq_ref, ck_ref, cv_ref, 2)):
        p = _mm(h, w_ref[:, off:off + B_QKV])
        off += B_QKV
        for j in range(2):
            t = p[:, j * LANES:(j + 1) * LANES]
            q_ref[0, :, j * LANES:(j + 1) * LANES] = (
                _norm_rope(t, gn[gi:gi + 1], cos, sin, lo, swap_lo) * scale).astype(BF16)
        k_ref[0] = _norm_rope(p[:, 2 * LANES:3 * LANES], gn[gi + 1:gi + 2], cos, sin, lo, swap_lo).astype(BF16)
        v_ref[0] = p[:, 3 * LANES:].astype(BF16)


def _in_proj(ctx_arr, lat_arr, mod, g, w, alog_l, dtb_l, cos_t, sin_t, gains):
    bsz = ctx_arr.shape[0]
    tt = cos_t.shape[0]
    nt = tt // TOK_TILE
    tile = lambda n: pl.BlockSpec((1, TOK_TILE, n), lambda b, t: (b, t, 0))
    tab = pl.BlockSpec((TOK_TILE, LANES), lambda b, t: (t, 0))
    vec = pl.BlockSpec((1, LANES), lambda b, t: (0, 0))
    f32 = lambda n: jax.ShapeDtypeStruct((bsz, tt, n), F32)
    b16 = lambda n: jax.ShapeDtypeStruct((bsz, tt, n), BF16)
    return pl.pallas_call(
        _inproj_kernel,
        grid=(bsz, nt),
        in_specs=_stream_specs(ctx_arr, lat_arr)
                 + [pl.BlockSpec((1, N_MOD, D_MODEL), lambda b, t: (_mod_row(b, t), 0, 0)),
                    pl.BlockSpec((1, D_MODEL), lambda b, t: (0, 0)),
                    pl.BlockSpec((D_MODEL, IN_COLS), lambda b, t: (0, 0)), vec, vec,
                    tab, tab, pl.BlockSpec((8, LANES), lambda b, t: (0, 0))],
        out_specs=[tile(A_QKV), tile(A_Z), tile(LANES), tile(LANES), tile(LANES)]
                  + [tile(256), tile(LANES), tile(LANES)] * 2,
        out_shape=[f32(A_QKV), f32(A_Z), f32(LANES), f32(LANES), f32(LANES)]
                  + [b16(256), b16(LANES), b16(LANES)] * 2,
        compiler_params=_params(("parallel", "arbitrary")),
    )(ctx_arr, lat_arr, mod, g, w, alog_l, dtb_l, cos_t, sin_t, gains)


def _gdn_kernel(q_ref, k_ref, v_ref, beta_ref, gc_ref, tot_ref, z_ref, cwq_ref, cwk_ref, cwv_ref, ng_ref, o_ref,
                qs, ks, vs, bs, gs, ts, grs, oacc, mq, bns, ext, *, n_ctx_chunks, n_chunks):
    hp = pl.program_id(1)
    tt = n_chunks * CHUNK
    lane = lax.broadcasted_iota(jnp.int32, (1, LANES), 1)
    lo = lane < HEAD_DIM
    shift = (LANES - 8 * hp) % LANES
    ri = lax.broadcasted_iota(jnp.int32, (LANES, LANES), 0)
    ci = lax.broadcasted_iota(jnp.int32, (LANES, LANES), 1)
    bd = (ri < HEAD_DIM) == (ci < HEAD_DIM)
    ones_bd = jnp.where(bd, 1.0, 0.0).astype(BF16)

    def prep_body(i, carry):
        for u in range(PREP_UNROLL):
            c = i * PREP_UNROLL + u
            r0 = pl.multiple_of(c * CHUNK, CHUNK)
            c8 = pl.multiple_of(c * 8, 8)
            first = jnp.logical_or(c == 0, c == n_ctx_chunks)
            last = jnp.logical_or(c == n_ctx_chunks - 1, c == n_chunks - 1)
            for i3, (src, cw_ref, dst, norm) in enumerate(((q_ref, cwq_ref, qs, True),
                                                           (k_ref, cwk_ref, ks, True),
                                                           (v_ref, cwv_ref, vs, False))):
                e = 3 * u + i3
                prev = src[0, pl.ds(jnp.maximum(r0 - 8, 0), 8), :]
                nxt = src[0, pl.ds(jnp.minimum(r0 + CHUNK, tt - 8), 8), :]
                ext[e, 0:8, :] = jnp.where(first, 0.0, prev)
                ext[e, 8:8 + CHUNK, :] = src[0, pl.ds(r0, CHUNK), :]
                ext[e, 8 + CHUNK:16 + CHUNK, :] = jnp.where(last, 0.0, nxt)
                cw = cw_ref[...]
                y = ext[e, 6:6 + CHUNK, :] * cw[0:1]
                for tap in range(1, GDN_CONV):
                    y = y + ext[e, 6 + tap:6 + tap + CHUNK, :] * cw[tap:tap + 1]
                y = y * _sigmoid(y)
                if norm:
                    y = y * _pair_sumsq_rsqrt_mxu(y, ones_bd, 1e-6, 1.0)
                dst[pl.ds(r0, CHUNK), :] = y
            bs[pl.ds(r0, CHUNK), :] = pltpu.roll(beta_ref[0, pl.ds(r0, CHUNK), :], shift, 1)
            gcr = pltpu.roll(gc_ref[0, pl.ds(r0, CHUNK), :], shift, 1)
            gs[pl.ds(r0, CHUNK), :] = gcr
            grs[pl.ds(c8, 8), :] = jnp.concatenate([gcr, gcr], axis=0).T[0:8, :]
            ts[pl.ds(c8, 8), :] = pltpu.roll(tot_ref[0, pl.ds(r0, 8), :], shift, 1)
        return carry

    lax.fori_loop(0, n_chunks // PREP_UNROLL, prep_body, 0)

    ti =lax.broadcasted_iota(jnp.int32, (CHUNK, LANES), 0)
    tj = lax.broadcasted_iota(jnp.int32, (CHUNK, LANES), 1) % CHUNK
    blk8 = (ti // 8) == (tj // 8)
    merge_masks = [jnp.logical_and((ti // (2 * m)) == (tj // (2 * m)), (ti // m) != (tj // m))
                   for m in (8, 16, 32)]
    masks = [((ti >= tj), (ti > tj)), ((ti <= tj), (ti < tj))]

    def blockdiag(x):
        xb = x.astype(BF16)
        return jnp.where(bd, jnp.concatenate([xb, xb], axis=0), jnp.zeros((LANES, LANES), BF16))

    def chunk_par(c, d):
        incl, strict = masks[d]
        r0 = pl.multiple_of(c * CHUNK, CHUNK)
        ib, ig = 2 * d, 4 + 2 * d
        bt = bs[pl.ds(r0, CHUNK), :]
        gt = gs[pl.ds(r0, CHUNK), :]
        bcol = jnp.where(lo, bt[:, ib:ib + 1], bt[:, ib + 1:ib + 2])
        gcol = jnp.where(lo, gt[:, ig:ig + 1], gt[:, ig + 1:ig + 2])
        k = ks[pl.ds(r0, CHUNK), :]
        q = qs[pl.ds(r0, CHUNK), :] * (HEAD_DIM ** -0.5)
        x = _nt(jnp.concatenate([k * bcol, q], axis=0).astype(BF16), blockdiag(k))
        yield
        gr8 = grs[pl.ds(pl.multiple_of(c * 8, 8), 8), :]
        grow = jnp.where(lo, gr8[ig:ig + 1, :], gr8[ig + 1:ig + 2, :])
        dec = jnp.exp(jnp.where(incl, gcol - grow, NEG_BIG))
        low = jnp.where(strict, x[:CHUNK] * dec, 0.0)
        intra = (x[CHUNK:] * dec).astype(BF16)
        l0 = jnp.where(blk8, low, 0.0)
        p = _mm(l0.astype(BF16), blockdiag(l0))
        yield
        n = -l0
        y = _mm(jnp.concatenate([n, p], axis=0).astype(BF16), blockdiag(p))
        yield
        n = n + p + y[:CHUNK]
        p = y[CHUNK:]
        y = _mm(n.astype(BF16), blockdiag(p))
        yield
        n = n + p + y
        for cmask in merge_masks:
            cm = jnp.where(cmask, low, 0.0)
            y = _mm(n.astype(BF16), blockdiag(cm))
            yield
            tc = cm + y
            y = _mm(tc.astype(BF16), blockdiag(n))
            yield
            n = n - tc - y
        vb = vs[pl.ds(r0, CHUNK), :] * bcol
        kbg = k * bcol * jnp.exp(gcol)
        y = _mm(n.astype(BF16), jnp.concatenate([blockdiag(vb), blockdiag(kbg)], axis=1))
        yield
        u = vb + y[:, :LANES]
        w = kbg + y[:, LANES:]
        bu, bw = blockdiag(u), blockdiag(w)
        iuw = _mm(intra, jnp.concatenate([bu, bw], axis=1))
        tl8 = ts[pl.ds(pl.multiple_of(c * 8, 8), 8), :]
        tl = jnp.where(lo, tl8[0:1, ig:ig + 1], tl8[0:1, ig + 1:ig + 2])
        ke = k * jnp.exp(tl - gcol)
        kett = jnp.concatenate([ke, ke], axis=0).T
        ket = jnp.where(lo, kett[:CHUNK], kett[CHUNK:]).astype(BF16)
        mb = _mm(ket, jnp.concatenate([bw, bu], axis=1))
        yield
        qe = q * jnp.exp(gcol) - iuw[:, LANES:]
        return jnp.concatenate([qe, mb[:, :LANES]], axis=0).astype(BF16), mb[:, LANES:], iuw[:, :LANES]

    def interleave(gens):
        res = [None] * len(gens)
        live = list(range(len(gens)))
        while live:
            for idx in list(live):
                try:
                    next(gens[idx])
                except StopIteration as stop:
                    res[idx] = stop.value
                    live.remove(idx)
        return res

    def scan_chunk(p, d):
        if d == 0:
            return p
        return jnp.where(p < n_ctx_chunks, n_ctx_chunks - 1 - p, n_chunks + n_ctx_chunks - 1 - p)

    def seq_chain(g, d, s):
        ig = 4 + 2 * d
        for j in range(PAR_GROUP):
            c = scan_chunk(g * PAR_GROUP + j, d)
            r0 = pl.multiple_of(c * CHUNK, CHUNK)
            tl8 = ts[pl.ds(pl.multiple_of(c * 8, 8), 8), :]
            arow = jnp.exp(jnp.where(lo, tl8[0:1, ig:ig + 1], tl8[0:1, ig + 1:ig + 2]))
            y = _mm(mq[d, c], blockdiag(s))
            yield
            oacc[d, pl.ds(r0, CHUNK), :] = oacc[d, pl.ds(r0, CHUNK), :] + y[:CHUNK]
            s = arow * s - y[CHUNK:] + bns[d, c]
        return s

    def group_step(g, states, with_par):
        gens, cs = [], []
        if with_par:
            cs = [(scan_chunk(g * PAR_GROUP + j, d), d) for j in range(PAR_GROUP) for d in range(2)]
            gens = [chunk_par(c, d) for c, d in cs]
        if states is not None:
            gens += [seq_chain(g - 1, d, states[d]) for d in range(2)]
        res = interleave(gens)
        for (c, d), (qm, bn, iu) in zip(cs, res):
            mq[d, c] = qm
            bns[d, c] = bn
            oacc[d, pl.ds(pl.multiple_of(c * CHUNK, CHUNK), CHUNK), :] = iu
        return None if states is None else (res[-2], res[-1])

    n_groups = n_chunks // PAR_GROUP
    zero = jnp.zeros((CHUNK, LANES), F32)
    group_step(0, None, True)
    states = lax.fori_loop(1, n_groups, lambda g, st: group_step(g, st, True), (zero, zero))
    group_step(n_groups, states, False)

    ng = ng_ref[...]

    def fin(i, carry):
        for u in range(FIN_UNROLL):
            r0 = pl.multiple_of((i * FIN_UNROLL + u) * CHUNK, CHUNK)
            o = oacc[0, pl.ds(r0, CHUNK), :] + oacc[1, pl.ds(r0, CHUNK), :]
            z = z_ref[0, pl.ds(r0, CHUNK), :]
            y = o * _pair_sumsq_rsqrt_mxu(o, ones_bd, RMS_EPS, float(HEAD_DIM)) * ng
            o_ref[0, pl.ds(r0, CHUNK), :] = (y * (z * _sigmoid(z))).astype(o_ref.dtype)
        return carry

    lax.fori_loop(0, n_chunks // FIN_UNROLL, fin, 0)


def _gdn(a_qkv, beta, gcum, gtot, a_z, cw8, ng2, n_ctx):
    bsz, tt, _ = a_qkv.shape
    n_pairs = GDN_HEADS // 2
    tok = lambda off: pl.BlockSpec((1, tt, LANES), lambda b, h: (b, 0, h + off))
    par = lambda off: pl.BlockSpec((8, LANES), lambda b, h: (0, h + off))
    gate = pl.BlockSpec((1, tt, LANES), lambda b, h: (b, 0, 0))
    kern = functools.partial(_gdn_kernel, n_ctx_chunks=n_ctx // CHUNK, n_chunks=tt // CHUNK)
    return pl.pallas_call(
        kern,
        grid=(bsz, n_pairs),
        in_specs=[tok(0), tok(n_pairs), tok(2 * n_pairs), gate, gate, gate, tok(0),
                  par(0), par(n_pairs), par(2 * n_pairs), pl.BlockSpec((1, LANES), lambda b, h: (0, 0))],
        out_specs=tok(0),
        out_shape=jax.ShapeDtypeStruct((bsz, tt, A_Z), BF16),
        scratch_shapes=[pltpu.VMEM((tt, LANES), F32)] * 5
                       + [pltpu.VMEM((tt // CHUNK * 8, LANES), F32)] * 2
                       + [pltpu.VMEM((2, tt, LANES), F32),
                          pltpu.VMEM((2, tt // CHUNK, 2 * CHUNK, LANES), BF16),
                          pltpu.VMEM((2, tt // CHUNK, CHUNK, LANES), F32),
                          pltpu.VMEM((3 * PREP_UNROLL, CHUNK + 16, LANES), F32)],
        compiler_params=_params(("parallel", "parallel")),
    )(a_qkv, a_qkv, a_qkv, beta, gcum, gtot, a_z, cw8, cw8, cw8, ng2)


VT_STEP = 256
VT_ROWS = LANES


def _head_queries(q_ref, lo):
    qms = []
    for tile in range(2):
        qt = q_ref[0, :, tile * LANES:(tile + 1) * LANES]
        for hh in range(2):
            qms.append(jnp.where(lo if hh == 0 else jnp.logical_not(lo), qt, jnp.zeros_like(qt)))
    return qms


def _fill_vt(v_ref, vt):
    for r in range(0, v_ref.shape[1], VT_STEP):
        vt[:, r:r + VT_STEP] = v_ref[0, r:r + VT_STEP, :].astype(F32).T.astype(BF16)


def _softmax_values(score_fns, vv_t, extra=None):
    scores = [fn() for fn in score_fns]
    probs, rinv = [], []
    for i, st in enumerate(scores):
        mx = jnp.max(st, axis=0, keepdims=True)
        if extra is not None:
            mx = jnp.maximum(mx, extra[i])
        p = jnp.exp2(st - mx)
        l = jnp.sum(p, axis=0, keepdims=True)
        if extra is not None:
            l = l + jnp.exp2(extra[i] - mx)
        rinv.append(1.0 / l)
        probs.append(p.astype(BF16))
    return [(_mm(vv_t, p) * r).T for p, r in zip(probs, rinv)]


def _store_heads(o_ref, outs, lo):
    for tile in range(2):
        o_ref[0, :, tile * LANES:(tile + 1) * LANES] = jnp.where(
            lo, outs[2 * tile], outs[2 * tile + 1]).astype(o_ref.dtype)


def _gattn_kernel(q_ref, k_ref, v_ref, o_ref, vt, *, first_tile, n_ctx):
    t = pl.program_id(1) + first_tile
    lane = lax.broadcasted_iota(jnp.int32, (1, LANES), 1)
    lo = lane < HEAD_DIM

    @pl.when(pl.program_id(1) == 0)
    def _():
        _fill_vt(v_ref, vt)

    def attend(n_keys):
        kk = k_ref[0, 0:n_keys, :]
        scores_t = [functools.partial(_nt, kk, qm) for qm in _head_queries(q_ref, lo)]
        _store_heads(o_ref, _softmax_values(scores_t, vt[:, 0:n_keys]), lo)

    if first_tile == 0:
        @pl.when(t == 0)
        def _():
            attend(n_ctx)

    @pl.when(t > 0)
    def _():
        attend(k_ref.shape[1])


def _global_attn(q, k, v, n_ctx, need_ctx):
    bsz, tt, _ = q.shape
    first = 0 if need_ctx else n_ctx // TOK_TILE
    nt = tt // TOK_TILE - first
    kv = pl.BlockSpec((1, tt, LANES), lambda b, t: (b, 0, 0))
    return pl.pallas_call(
        functools.partial(_gattn_kernel, first_tile=first, n_ctx=n_ctx),
        grid=(bsz, nt),
        in_specs=[pl.BlockSpec((1, TOK_TILE, 256), lambda b, t: (b, t + first, 0)), kv, kv],
        out_specs=pl.BlockSpec((1, TOK_TILE, 256), lambda b, t: (b, t, 0)),
        out_shape=jax.ShapeDtypeStruct((bsz, nt * TOK_TILE, 256), BF16),
        scratch_shapes=[pltpu.VMEM((VT_ROWS, tt), BF16)],
        compiler_params=_params(("arbitrary", "arbitrary")),
    )(q, k, v)


def _wattn_kernel(q_ref, k_ref, v_ref, sink_ref, o_ref, vt, *, first_tile, n_ctx, n_lat):
    t = pl.program_id(1) + first_tile
    n_ctx_tiles = n_ctx // WQ_TILE
    lane = lax.broadcasted_iota(jnp.int32, (1, LANES), 1)
    lo = lane < HEAD_DIM
    kw = WQ_TILE + 2 * WINDOW
    sink = sink_ref[...]
    sinks = [sink[2 * hh + tile:2 * hh + tile + 1, 0:1] * LOG2E for tile in range(2) for hh in range(2)]

    @pl.when(pl.program_id(1) == 0)
    def _():
        _fill_vt(v_ref, vt)

    def attend(win):
        qms = _head_queries(q_ref, lo)
        if win:
            n = t - n_ctx_tiles
            ws = jnp.clip(n * WQ_TILE - WINDOW, 0, n_lat - kw)
            start = pl.multiple_of(n_ctx + ws, WINDOW)
            kk = jnp.concatenate([k_ref[0, 0:n_ctx, :], k_ref[0, pl.ds(start, kw), :]], axis=0)
            vv = jnp.concatenate([vt[:, 0:n_ctx], vt[:, pl.ds(start, kw)]], axis=1)
            row = lax.broadcasted_iota(jnp.int32, (n_ctx + kw, WQ_TILE), 0)
            qpos = n * WQ_TILE + lax.broadcasted_iota(jnp.int32, (n_ctx + kw, WQ_TILE), 1)
            valid = jnp.logical_or(row < n_ctx, jnp.abs(ws + row - n_ctx - qpos) <= WINDOW)
            scores_t = [lambda qm=qm: jnp.where(valid, _nt(kk, qm), NEG_BIG) for qm in qms]
        else:
            kk = k_ref[0, 0:n_ctx, :]
            vv = vt[:, 0:n_ctx]
            scores_t = [functools.partial(_nt, kk, qm) for qm in qms]
        _store_heads(o_ref, _softmax_values(scores_t, vv, sinks), lo)

    if first_tile == 0:
        @pl.when(t < n_ctx_tiles)
        def _():
            attend(False)

    @pl.when(t >= n_ctx_tiles)
    def _():
        attend(True)


def _window_attn(q, k, v, sink8, n_ctx, need_ctx):
    bsz, tt, _ = q.shape
    first = 0 if need_ctx else n_ctx // WQ_TILE
    nt = tt // WQ_TILE - first
    kv = pl.BlockSpec((1, tt, LANES), lambda b, t: (b, 0, 0))
    return pl.pallas_call(
        functools.partial(_wattn_kernel, first_tile=first, n_ctx=n_ctx, n_lat=tt - n_ctx),
        grid=(bsz, nt),
        in_specs=[pl.BlockSpec((1, WQ_TILE, 256), lambda b, t: (b, t + first, 0)), kv, kv,
                  pl.BlockSpec((8, LANES), lambda b, t: (0, 0))],
        out_specs=pl.BlockSpec((1, WQ_TILE, 256), lambda b, t: (b, t, 0)),
        out_shape=jax.ShapeDtypeStruct((bsz, nt * WQ_TILE, 256), BF16),
        scratch_shapes=[pltpu.VMEM((VT_ROWS, tt), BF16)],
        compiler_params=_params(("arbitrary", "arbitrary")),
    )(q, k, v, sink8)


def _outmlp_kernel(xc_ref, xl_ref, oa_ref, ob_ref, oc_ref, mod_ref, g_ref, wo_ref, w1_ref, w2_ref, o_ref, *,
                   first_tile):
    m = mod_ref[0]
    mix = _mm(oa_ref[0], wo_ref[0:A_Z, :])
    mix = mix + _mm(ob_ref[0], wo_ref[A_Z:A_Z + 256, :])
    mix = mix + _mm(oc_ref[0], wo_ref[A_Z + 256:A_Z + 512, :])
    x = _stream_tile(xc_ref, xl_ref, pl.program_id(1) + first_tile) + m[2:3] * mix
    h = _modulated(x, g_ref[...], m[3:4], m[4:5]).astype(BF16)
    fc = 1024
    acc = None
    for f in range(D_FF // fc):
        a = jnp.maximum(_mm(h, w1_ref[:, f * fc:(f + 1) * fc]), 0.0)
        y = _mm((a * a).astype(BF16), w2_ref[f * fc:(f + 1) * fc, :])
        acc = y if acc is None else acc + y
    o_ref[0] = x + m[5:6] * acc


def _out_mlp(ctx_arr, lat_arr, oa, ob, oc, mod, g, wo, w1, w2, n_ctx, need_ctx):
    bsz, tt, _ = oa.shape
    first = 0 if need_ctx else n_ctx // TOK_TILE
    nt = tt // TOK_TILE - first
    att = pl.BlockSpec((1, TOK_TILE, 256), lambda b, t: (b, t, 0))
    full = lambda r, c: pl.BlockSpec((r, c), lambda b, t: (0, 0))
    return pl.pallas_call(
        functools.partial(_outmlp_kernel, first_tile=first),
        grid=(bsz, nt),
        in_specs=_stream_specs(ctx_arr, lat_arr, first)
                 + [pl.BlockSpec((1, TOK_TILE, A_Z), lambda b, t: (b, t + first, 0)), att, att,
                    pl.BlockSpec((1, N_MOD, D_MODEL), lambda b, t: (_mod_row(b, t + first), 0, 0)),
                    full(1, D_MODEL), full(D_MODEL, D_MODEL), full(D_MODEL, D_FF), full(D_FF, D_MODEL)],
        out_specs=pl.BlockSpec((1, TOK_TILE, D_MODEL), lambda b, t: (b, t, 0)),
        out_shape=jax.ShapeDtypeStruct((bsz, nt * TOK_TILE, D_MODEL), F32),
        compiler_params=_params(("parallel", "arbitrary")),
    )(ctx_arr, lat_arr, oa, ob, oc, mod, g, wo, w1, w2)


_HEAD_PERM = (0, 2, 1, 3)


def _gate_param_lanes(p):
    out = jnp.zeros((LANES,), F32)
    idx, src = [], []
    for pair in range(GDN_HEADS // 2):
        for d in range(2):
            for j in range(2):
                idx.append(pair * 8 + 4 + d * 2 + j)
                src.append(d * GDN_HEADS + 2 * pair + j)
    return out.at[jnp.array(idx)].set(p.reshape(-1)[jnp.array(src)]).reshape(1, LANES)


def _rope_tables(n_ctx, n_lat):
    rows = n_lat // GRID_W
    row = jnp.repeat(jnp.arange(rows, dtype=F32), GRID_W)
    col = jnp.tile(jnp.arange(GRID_W, dtype=F32), rows)
    half = HEAD_DIM // 4
    inv_freq = ROPE_THETA ** (-jnp.arange(half, dtype=F32) / half)
    ang_r = row[:, None] * inv_freq
    ang_c = col[:, None] * inv_freq
    cr, sr, cc, sc = jnp.cos(ang_r), jnp.sin(ang_r), jnp.cos(ang_c), jnp.sin(ang_c)
    cos = jnp.concatenate([cr, cr, cc, cc], axis=-1)
    sin = jnp.concatenate([-sr, sr, -sc, sc], axis=-1)
    cos = jnp.concatenate([jnp.ones((n_ctx, HEAD_DIM), F32), cos], axis=0)
    sin = jnp.concatenate([jnp.zeros((n_ctx, HEAD_DIM), F32), sin], axis=0)
    return jnp.tile(cos, (1, 2)), jnp.tile(sin, (1, 2))


def _pad_rows(a, rows):
    return jnp.concatenate([a, jnp.zeros((rows - a.shape[0],) + a.shape[1:], a.dtype)], axis=0)


def kernel(x, c, ctx, c_ctx, w_mod, b_mod, g_attn, w_in, gdn_conv_w, gdn_a_log, gdn_dt_bias, gdn_norm_g,
           ga_q_norm_g, ga_k_norm_g, wa_q_norm_g, wa_k_norm_g, wa_sink, w_out, g_mlp, w_mlp_in, w_mlp_out):
    bsz, n_lat, _ = x.shape
    n_ctx = ctx.shape[1]
    depth = w_mod.shape[0]
    assert bsz <= 8 and n_ctx == TOK_TILE and n_lat % TOK_TILE == 0

    ctx_arr, lat_arr = ctx, x
    cond16 = _pad_rows(jnp.concatenate([_pad_rows(c, 8), c_ctx[None, :]], axis=0), 16)
    cos_t, sin_t = _rope_tables(n_ctx, n_lat)

    b0 = A_QKV + A_Z + 2 * A_GATES
    n_pairs = GDN_HEADS // 2

    def head_perm(a, axis):
        blocks = [lax.slice_in_dim(a, h * HEAD_DIM, (h + 1) * HEAD_DIM, axis=axis) for h in _HEAD_PERM]
        return jnp.concatenate(blocks, axis=axis)

    for l in range(depth):
        need_ctx = l < depth - 1
        mod = _ada_mod(cond16, w_mod[l], b_mod[l]).reshape(16, N_MOD, D_MODEL)
        w = w_in[l]
        wg = w[:, A_QKV + A_Z:b0].reshape(D_MODEL, 2, 2, n_pairs, 2).transpose(0, 3, 1, 2, 4).reshape(D_MODEL, -1)
        parts = [w[:, :A_QKV + A_Z], wg, jnp.zeros((D_MODEL, LANES - 2 * A_GATES), F32)]
        for base in (b0, b0 + B_QKV):
            parts += [head_perm(w[:, base:base + 256], 1), w[:, base + 256:base + B_QKV]]
        w_in_l = jnp.concatenate(parts, axis=1).astype(BF16)
        gains = _pad_rows(jnp.stack([jnp.tile(g, 2) for g in
                                     (ga_q_norm_g[l], ga_k_norm_g[l], wa_q_norm_g[l], wa_k_norm_g[l])]), 8)
        a_qkv, a_z, beta, gcum, gtot, bq, bk, bv, cq, ck, cv = _in_proj(
            ctx_arr, lat_arr, mod, g_attn[l].reshape(1, D_MODEL), w_in_l, _gate_param_lanes(gdn_a_log[l]),
            _gate_param_lanes(gdn_dt_bias[l]), cos_t, sin_t, gains)

        o_a = _gdn(a_qkv, beta, gcum, gtot, a_z, _pad_rows(gdn_conv_w[l], 8),
                   jnp.tile(gdn_norm_g[l], 2).reshape(1, LANES), n_ctx)

        o_b = _global_attn(bq, bk, bv, n_ctx, need_ctx)
        sink8 = _pad_rows(jnp.broadcast_to(wa_sink[l][:, None], (WA_HEADS, LANES)), 8)
        o_c = _window_attn(cq, ck, cv, sink8, n_ctx, need_ctx)

        wo = jnp.concatenate([w_out[l][:A_Z], head_perm(w_out[l][A_Z:A_Z + 256], 0),
                              head_perm(w_out[l][A_Z + 256:], 0)], axis=0).astype(BF16)
        xt = _out_mlp(ctx_arr, lat_arr, o_a, o_b, o_c, mod, g_mlp[l].reshape(1, D_MODEL), wo,
                      w_mlp_in[l].astype(BF16), w_mlp_out[l].astype(BF16), n_ctx, need_ctx)
        ctx_arr = lat_arr = xt
    return xt
```

```python
import functools

import jax
import jax.numpy as jnp
from jax import lax
from jax.experimental import pallas as pl
from jax.experimental.pallas import tpu as pltpu

F32 = jnp.float32
BF16 = jnp.bfloat16

D_MODEL = 1024
GRID_W = 64
HEAD_DIM = 64
ROPE_THETA = 10000.0
GDN_HEADS = 8
GDN_CONV = 5
CHUNK = 64
GA_HEADS = 4
WA_HEADS = 4
WINDOW = 128
D_FF = 4 * D_MODEL
N_MOD = 6
RMS_EPS = 1e-6
A_QKV = 1536
A_Z = 512
A_GATES = 16
B_QKV = 512
C_QKV = 512
LANES = 128
TOK_TILE = 256
WQ_TILE = 256
PREP_SLOTS = 6
PAR_GROUP = 9
PAR_STAGES = 14
NEG_BIG = -1e30
LOG2E = 1.4426950408889634
VMEM_LIMIT = 56 * 1024 * 1024


def _mm(a, b):
    return jnp.dot(a, b, preferred_element_type=F32)


def _nt(a, b):
    return lax.dot_general(a, b, (((1,), (1,)), ((), ())), preferred_element_type=F32)


def _sigmoid(x):
    return 1.0 / (1.0 + jnp.exp(-x))


def _params(sem):
    return pltpu.CompilerParams(dimension_semantics=sem, vmem_limit_bytes=VMEM_LIMIT)


def _mod_kernel(c_ref, w_ref, b_ref, o_ref):
    c = c_ref[...]
    s = c * _sigmoid(c)
    o_ref[...] = _mm(s.astype(BF16), w_ref[...].astype(BF16)) + b_ref[...]


def _ada_mod(cond16, w_mod, b_mod):
    n = N_MOD * D_MODEL
    tn = 1024
    return pl.pallas_call(
        _mod_kernel,
        grid=(n // tn,),
        in_specs=[pl.BlockSpec((16, D_MODEL), lambda j: (0, 0)),
                  pl.BlockSpec((D_MODEL, tn), lambda j: (0, j)),
                  pl.BlockSpec((1, tn), lambda j: (0, j))],
        out_specs=pl.BlockSpec((16, tn), lambda j: (0, j)),
        out_shape=jax.ShapeDtypeStruct((16, n), F32),
        compiler_params=_params(("arbitrary",)),
    )(cond16, w_mod, b_mod.reshape(1, n))


def _mod_row(b, t):
    return jnp.where(t == 0, 8, b)


def _modulated(x, g, shift, scale):
    ms = jnp.mean(x * x, axis=-1, keepdims=True)
    y = x * lax.rsqrt(ms + RMS_EPS) * g
    return y * (1.0 + scale) + shift


IN_COLS = A_QKV + A_Z + LANES + B_QKV + C_QKV


def _pair_sumsq_rsqrt(y, lo, eps, denom):
    yy = y * y
    s0 = jnp.sum(jnp.where(lo, yy, 0.0), axis=-1, keepdims=True)
    s1 = jnp.sum(jnp.where(lo, 0.0, yy), axis=-1, keepdims=True)
    return jnp.where(lo, lax.rsqrt(s0 * (1.0 / denom) + eps), lax.rsqrt(s1 * (1.0 / denom) + eps))


def _pair_sumsq_rsqrt_mxu(y, ones_bd, eps, denom):
    yy = y * y
    hi = yy.astype(BF16)
    lo_part = (yy - hi.astype(F32)).astype(BF16)
    ss = _mm(hi, ones_bd) + _mm(lo_part, ones_bd)
    return lax.rsqrt(ss * (1.0 / denom) + eps)


def _norm_rope(t, gain, cos, sin, lo, swap_lo):
    y = t * _pair_sumsq_rsqrt(t, lo, RMS_EPS, float(HEAD_DIM)) * gain
    sw = jnp.where(swap_lo, pltpu.roll(y, LANES - 16, 1), pltpu.roll(y, 16, 1))
    return y * cos + sw * sin


def _stream_specs(ctx_arr, lat_arr, first=0):
    lat_off = 1 if lat_arr is ctx_arr else 0
    return [pl.BlockSpec((1, TOK_TILE, D_MODEL), lambda b, t: (b, 0, 0)),
            pl.BlockSpec((1, TOK_TILE, D_MODEL), lambda b, t: (b, jnp.maximum(t + first - 1, 0) + lat_off, 0))]


def _stream_tile(xc_ref, xl_ref, t):
    return jnp.where(t == 0, xc_ref[0], xl_ref[0])


def _inproj_kernel(xc_ref, xl_ref, mod_ref, g_ref, w_ref, alog_ref, dtb_ref, cos_ref, sin_ref, gn_ref,
                   oa_ref, oz_ref, ob_ref, og_ref, ot_ref, bq_ref, bk_ref, bv_ref, cq_ref, ck_ref, cv_ref):
    m = mod_ref[0]
    x = _stream_tile(xc_ref, xl_ref, pl.program_id(1))
    h = _modulated(x, g_ref[...], m[0:1], m[1:2]).astype(BF16)
    lane = lax.broadcasted_iota(jnp.int32, (1, LANES), 1)
    lo = lane < HEAD_DIM
    oa_ref[0] = _mm(h, w_ref[:, 0:A_QKV])
    off = A_QKV
    oz_ref[0] = _mm(h, w_ref[:, off:off + A_Z])
    off += A_Z

    gt = _mm(h, w_ref[:, off:off + LANES])
    off += LANES
    a = gt + dtb_ref[...]
    sp = jnp.maximum(a, 0.0) + jnp.log(1.0 + jnp.exp(-jnp.abs(a)))
    gl = -jnp.exp(alog_ref[...]) * sp
    g_hi = gl.astype(BF16)
    r1 = gl - g_hi.astype(F32)
    g_mid = r1.astype(BF16)
    g_lo = (r1 - g_mid.astype(F32)).astype(BF16)
    ri = lax.broadcasted_iota(jnp.int32, (TOK_TILE, TOK_TILE), 0)
    ci = lax.broadcasted_iota(jnp.int32, (TOK_TILE, TOK_TILE), 1)
    same = (ri // CHUNK) == (ci // CHUNK)
    blk = jnp.where(same, 1.0, 0.0).astype(BF16)
    tri = jnp.where(jnp.logical_and(same, ri >= ci), 1.0, 0.0).astype(BF16)
    gpre = _mm(tri, g_hi) + _mm(tri, g_mid) + _mm(tri, g_lo)
    tot = _mm(blk, g_hi) + _mm(blk, g_mid) + _mm(blk, g_lo)
    ob_ref[0] = _sigmoid(gt)
    og_ref[0] = jnp.where((lane % 4) >= 2, tot - gpre + gl, gpre)
    ot_ref[0] = tot

    swap_lo = (lane % 32) < 16
    cos = cos_ref[...]
    sin = sin_ref[...]
    gn = gn_ref[...]
    scale = HEAD_DIM ** -0.5 * LOG2E
    for q_ref, k_ref, v_ref, gi in ((bq_ref, bk_ref, bv_ref, 0), (cq_ref, ck_ref, cv_ref, 2)):
        p = _mm(h, w_ref[:, off:off + B_QKV])
        off += B_QKV
        for j in range(2):
            t = p[:, j * LANES:(j + 1) * LANES]
            q_ref[0, :, j * LANES:(j + 1) * LANES] = (
                _norm_rope(t, gn[gi:gi + 1], cos, sin, lo, swap_lo) * scale).astype(BF16)
        k_ref[0] = _norm_rope(p[:, 2 * LANES:3 * LANES], gn[gi + 1:gi + 2], cos, sin, lo, swap_lo).astype(BF16)
        v_ref[0] = p[:, 3 * LANES:].astype(BF16)


def _in_proj(ctx_arr, lat_arr, mod, g, w, alog_l, dtb_l, cos_t, sin_t, gains):
    bsz = ctx_arr.shape[0]
    tt = cos_t.shape[0]
    nt = tt // TOK_TILE
    tile = lambda n: pl.BlockSpec((1, TOK_TILE, n), lambda b, t: (b, t, 0))
    tab = pl.BlockSpec((TOK_TILE, LANES), lambda b, t: (t, 0))
    vec = pl.BlockSpec((1, LANES), lambda b, t: (0, 0))
    f32 = lambda n: jax.ShapeDtypeStruct((bsz, tt, n), F32)
    b16 = lambda n: jax.ShapeDtypeStruct((bsz, tt, n), BF16)
    return pl.pallas_call(
        _inproj_kernel,
        grid=(bsz, nt),
        in_specs=_stream_specs(ctx_arr, lat_arr)
                 + [pl.BlockSpec((1, N_MOD, D_MODEL), lambda b, t: (_mod_row(b, t), 0, 0)),
                    pl.BlockSpec((1, D_MODEL), lambda b, t: (0, 0)),
                    pl.BlockSpec((D_MODEL, IN_COLS), lambda b, t: (0, 0)), vec, vec,
                    tab, tab, pl.BlockSpec((8, LANES), lambda b, t: (0, 0))],
        out_specs=[tile(A_QKV), tile(A_Z), tile(LANES), tile(LANES), tile(LANES)]
                  + [tile(256), tile(LANES), tile(LANES)] * 2,
        out_shape=[f32(A_QKV), f32(A_Z), f32(LANES), f32(LANES), f32(LANES)]
                  + [b16(256), b16(LANES), b16(LANES)] * 2,
        compiler_params=_params(("parallel", "arbitrary")),
    )(ctx_arr, lat_arr, mod, g, w, alog_l, dtb_l, cos_t, sin_t, gains)


def _gdn_kernel(q_ref, k_ref, v_ref, beta_ref, gc_ref, tot_ref, z_ref, cwq_ref, cwk_ref, cwv_ref, ng_ref, o_ref,
                qs, ks, vs, bs, gs, ts, grs, oacc, mq, bns, ext, *, n_ctx_chunks, n_chunks):
    hp = pl.program_id(1)
    tt = n_chunks * CHUNK
    lane = lax.broadcasted_iota(jnp.int32, (1, LANES), 1)
    lo = lane < HEAD_DIM
    shift = (LANES - 8 * hp) % LANES
    ri = lax.broadcasted_iota(jnp.int32, (LANES, LANES), 0)
    ci = lax.broadcasted_iota(jnp.int32, (LANES, LANES), 1)
    bd = (ri < HEAD_DIM) == (ci < HEAD_DIM)
    ones_bd = jnp.where(bd, 1.0, 0.0).astype(BF16)

    zeros8 = jnp.zeros((8, LANES), F32)

    def prep_chunk(c, slot):
        r0 = c * CHUNK
        first = c == 0 or c == n_ctx_chunks
        last = c == n_ctx_chunks - 1 or c == n_chunks - 1
        for i3, (src, cw_ref, dst, norm) in enumerate(((q_ref, cwq_ref, qs, True),
                                                       (k_ref, cwk_ref, ks, True),
                                                       (v_ref, cwv_ref, vs, False))):
            e = 3 * slot + i3
            ext[e, 0:8, :] = zeros8 if first else src[0, r0 - 8:r0, :]
            ext[e, 8:8 + CHUNK, :] = src[0, r0:r0 + CHUNK, :]
            ext[e, 8 + CHUNK:16 + CHUNK, :] = zeros8 if last else src[0, r0 + CHUNK:r0 + CHUNK + 8, :]
            cw = cw_ref[...]
            y = ext[e, 6:6 + CHUNK, :] * cw[0:1]
            for tap in range(1, GDN_CONV):
                y = y + ext[e, 6 + tap:6 + tap + CHUNK, :] * cw[tap:tap + 1]
            y = y * _sigmoid(y)
            if norm:
                y = y * _pair_sumsq_rsqrt_mxu(y, ones_bd, 1e-6, 1.0)
            dst[r0:r0 + CHUNK, :] = y
        bs[r0:r0 + CHUNK, :] = pltpu.roll(beta_ref[0, r0:r0 + CHUNK, :], shift, 1)
        gcr = pltpu.roll(gc_ref[0, r0:r0 + CHUNK, :], shift, 1)
        gs[r0:r0 + CHUNK, :] = gcr
        grs[c * 8:c * 8 + 8, :] = jnp.concatenate([gcr, gcr], axis=0).T[0:8, :]
        ts[c * 8:c * 8 + 8, :] = pltpu.roll(tot_ref[0, r0:r0 + 8, :], shift, 1)

    prep_count = [0]

    def prep_gen(chunks, per_round):
        for i, c in enumerate(chunks):
            prep_chunk(c, prep_count[0] % PREP_SLOTS)
            prep_count[0] += 1
            if i % per_round == per_round - 1:
                yield

    ti =lax.broadcasted_iota(jnp.int32, (CHUNK, LANES), 0)
    tj = lax.broadcasted_iota(jnp.int32, (CHUNK, LANES), 1) % CHUNK
    blk8 = (ti // 8) == (tj // 8)
    merge_masks = [jnp.logical_and((ti // (2 * m)) == (tj // (2 * m)), (ti // m) != (tj // m))
                   for m in (8, 16, 32)]
    masks = [((ti >= tj), (ti > tj)), ((ti <= tj), (ti < tj))]

    def blockdiag(x):
        xb = x.astype(BF16)
        return jnp.where(bd, jnp.concatenate([xb, xb], axis=0), jnp.zeros((LANES, LANES), BF16))

    def chunk_par(c, d):
        incl, strict = masks[d]
        r0 = c * CHUNK
        ib, ig = 2 * d, 4 + 2 * d
        bt = bs[pl.ds(r0, CHUNK), :]
        gt = gs[pl.ds(r0, CHUNK), :]
        bcol = jnp.where(lo, bt[:, ib:ib + 1], bt[:, ib + 1:ib + 2])
        gcol = jnp.where(lo, gt[:, ig:ig + 1], gt[:, ig + 1:ig + 2])
        k = ks[pl.ds(r0, CHUNK), :]
        q = qs[pl.ds(r0, CHUNK), :] * (HEAD_DIM ** -0.5)
        x = _nt(jnp.concatenate([k * bcol, q], axis=0).astype(BF16), blockdiag(k))
        yield
        gr8 = grs[c * 8:c * 8 + 8, :]
        grow = jnp.where(lo, gr8[ig:ig + 1, :], gr8[ig + 1:ig + 2, :])
        dec = jnp.exp(jnp.where(incl, gcol - grow, NEG_BIG))
        low = jnp.where(strict, x[:CHUNK] * dec, 0.0)
        intra = (x[CHUNK:] * dec).astype(BF16)
        l0 = jnp.where(blk8, low, 0.0)
        p = _mm(l0.astype(BF16), blockdiag(l0))
        yield
        n = -l0
        y = _mm(jnp.concatenate([n, p], axis=0).astype(BF16), blockdiag(p))
        yield
        n = n + p + y[:CHUNK]
        p = y[CHUNK:]
        y = _mm(n.astype(BF16), blockdiag(p))
        yield
        n = n + p + y
        for cmask in merge_masks:
            cm = jnp.where(cmask, low, 0.0)
            y = _mm(n.astype(BF16), blockdiag(cm))
            yield
            tc = cm + y
            y = _mm(tc.astype(BF16), blockdiag(n))
            yield
            n = n - tc - y
        vb = vs[pl.ds(r0, CHUNK), :] * bcol
        kbg = k * bcol * jnp.exp(gcol)
        y = _mm(n.astype(BF16), jnp.concatenate([blockdiag(vb), blockdiag(kbg)], axis=1))
        yield
        u = vb + y[:, :LANES]
        w = kbg + y[:, LANES:]
        bu, bw = blockdiag(u), blockdiag(w)
        iuw = _mm(intra, jnp.concatenate([bu, bw], axis=1))
        tl8 = ts[c * 8:c * 8 + 8, :]
        tl = jnp.where(lo, tl8[0:1, ig:ig + 1], tl8[0:1, ig + 1:ig + 2])
        ke = k * jnp.exp(tl - gcol)
        kett = jnp.concatenate([ke, ke], axis=0).T
        ket = jnp.where(lo, kett[:CHUNK], kett[CHUNK:]).astype(BF16)
        mb = _mm(ket, jnp.concatenate([bw, bu], axis=1))
        yield
        qe = q * jnp.exp(gcol) - iuw[:, LANES:]
        return jnp.concatenate([qe, mb[:, :LANES]], axis=0).astype(BF16), mb[:, LANES:], iuw[:, :LANES]

    def interleave(gens):
        res = [None] * len(gens)
        live = list(range(len(gens)))
        while live:
            for idx in list(live):
                try:
                    next(gens[idx])
                except StopIteration as stop:
                    res[idx] = stop.value
                    live.remove(idx)
        return res

    def scan_chunk(p, d):
        if d == 0:
            return p
        return n_ctx_chunks - 1 - p if p < n_ctx_chunks else n_chunks + n_ctx_chunks - 1 - p

    def seq_chain(g, d, s):
        ig = 4 + 2 * d
        for j in range(PAR_GROUP):
            c = scan_chunk(g * PAR_GROUP + j, d)
            r0 = c * CHUNK
            tl8 = ts[c * 8:c * 8 + 8, :]
            arow = jnp.exp(jnp.where(lo, tl8[0:1, ig:ig + 1], tl8[0:1, ig + 1:ig + 2]))
            y = _mm(mq[d, c], blockdiag(s))
            yield
            oacc[d, r0:r0 + CHUNK, :] = oacc[d, r0:r0 + CHUNK, :] + y[:CHUNK]
            s = arow * s - y[CHUNK:] + bns[d, c]
        return s

    ng = ng_ref[...]

    def fin_gen(chunks, per_round):
        for i, c in enumerate(chunks):
            r0 = c * CHUNK
            o = oacc[0, r0:r0 + CHUNK, :] + oacc[1, r0:r0 + CHUNK, :]
            z = z_ref[0, r0:r0 + CHUNK, :]
            y = o * _pair_sumsq_rsqrt_mxu(o, ones_bd, RMS_EPS, float(HEAD_DIM)) * ng
            o_ref[0, r0:r0 + CHUNK, :] = (y * (z * _sigmoid(z))).astype(o_ref.dtype)
            if i % per_round == per_round - 1:
                yield

    n_groups = n_chunks // PAR_GROUP
    group_chunks = [[(scan_chunk(g * PAR_GROUP + j, d), d) for j in range(PAR_GROUP) for d in range(2)]
                    for g in range(n_groups)]
    prepared = set()

    def to_prepare(g):
        todo = sorted(set(c for c, _ in group_chunks[g]) - prepared) if g < n_groups else []
        prepared.update(todo)
        return todo

    for _ in prep_gen(to_prepare(0), 1):
        pass
    states = None
    zero = jnp.zeros((CHUNK, LANES), F32)
    for g in range(n_groups + 1):
        cs = group_chunks[g] if g < n_groups else []
        gens = [chunk_par(c, d) for c, d in cs]
        n_par = len(gens)
        if g > 0:
            gens += [seq_chain(g - 1, d, (zero, zero)[d] if states is None else states[d]) for d in range(2)]
        nxt = to_prepare(g + 1)
        if nxt:
            gens.append(prep_gen(nxt, -(-len(nxt) // PAR_STAGES)))
        if g == n_groups:
            pending = set(c for c, _ in group_chunks[n_groups - 1])
            ready = [c for c in range(n_chunks) if c not in pending]
            gens.append(fin_gen(ready, -(-len(ready) // PAR_GROUP)))
        res = interleave(gens)
        for (c, d), (qm, bn, iu) in zip(cs, res[:n_par]):
            mq[d, c] = qm
            bns[d, c] = bn
            oacc[d, c * CHUNK:(c + 1) * CHUNK, :] = iu
        if g > 0:
            states = (res[n_par], res[n_par + 1])
    for _ in fin_gen(sorted(set(c for c, _ in group_chunks[n_groups - 1])), 1):
        pass


def _gdn(a_qkv, beta, gcum, gtot, a_z, cw8, ng2, n_ctx):
    bsz, tt, _ = a_qkv.shape
    n_pairs = GDN_HEADS // 2
    tok = lambda off: pl.BlockSpec((1, tt, LANES), lambda b, h: (b, 0, h + off))
    par = lambda off: pl.BlockSpec((8, LANES), lambda b, h: (0, h + off))
    gate = pl.BlockSpec((1, tt, LANES), lambda b, h: (b, 0, 0))
    kern = functools.partial(_gdn_kernel, n_ctx_chunks=n_ctx // CHUNK, n_chunks=tt // CHUNK)
    return pl.pallas_call(
        kern,
        grid=(bsz, n_pairs),
        in_specs=[tok(0), tok(n_pairs), tok(2 * n_pairs), gate, gate, gate, tok(0),
                  par(0), par(n_pairs), par(2 * n_pairs), pl.BlockSpec((1, LANES), lambda b, h: (0, 0))],
        out_specs=tok(0),
        out_shape=jax.ShapeDtypeStruct((bsz, tt, A_Z), BF16),
        scratch_shapes=[pltpu.VMEM((tt, LANES), F32)] * 5
                       + [pltpu.VMEM((tt // CHUNK * 8, LANES), F32)] * 2
                       + [pltpu.VMEM((2, tt, LANES), F32),
                          pltpu.VMEM((2, tt // CHUNK, 2 * CHUNK, LANES), BF16),
                          pltpu.VMEM((2, tt // CHUNK, CHUNK, LANES), F32),
                          pltpu.VMEM((3 * PREP_SLOTS, CHUNK + 16, LANES), F32)],
        compiler_params=_params(("parallel", "parallel")),
    )(a_qkv, a_qkv, a_qkv, beta, gcum, gtot, a_z, cw8, cw8, cw8, ng2)


VT_STEP = 256
VT_ROWS = LANES


def _head_queries(q_ref, lo):
    qms = []
    for tile in range(2):
        qt = q_ref[0, :, tile * LANES:(tile + 1) * LANES]
        for hh in range(2):
            qms.append(jnp.where(lo if hh == 0 else jnp.logical_not(lo), qt, jnp.zeros_like(qt)))
    return qms


def _fill_vt(v_ref, vt):
    for r in range(0, v_ref.shape[1], VT_STEP):
        vt[:, r:r + VT_STEP] = v_ref[0, r:r + VT_STEP, :].astype(F32).T.astype(BF16)


def _softmax_values(score_fns, vv_t, extra=None):
    scores = [fn() for fn in score_fns]
    probs, rinv = [], []
    for i, st in enumerate(scores):
        mx = jnp.max(st, axis=0, keepdims=True)
        if extra is not None:
            mx = jnp.maximum(mx, extra[i])
        p = jnp.exp2(st - mx)
        l = jnp.sum(p, axis=0, keepdims=True)
        if extra is not None:
            l = l + jnp.exp2(extra[i] - mx)
        rinv.append(1.0 / l)
        probs.append(p.astype(BF16))
    return [(_mm(vv_t, p) * r).T for p, r in zip(probs, rinv)]


def _store_heads(o_ref, outs, lo):
    for tile in range(2):
        o_ref[0, :, tile * LANES:(tile + 1) * LANES] = jnp.where(
            lo, outs[2 * tile], outs[2 * tile + 1]).astype(o_ref.dtype)


def _gattn_kernel(q_ref, k_ref, v_ref, o_ref, vt, *, first_tile, n_ctx):
    t = pl.program_id(1) + first_tile
    lane = lax.broadcasted_iota(jnp.int32, (1, LANES), 1)
    lo = lane < HEAD_DIM

    @pl.when(pl.program_id(1) == 0)
    def _():
        _fill_vt(v_ref, vt)

    def attend(n_keys):
        kk = k_ref[0, 0:n_keys, :]
        scores_t = [functools.partial(_nt, kk, qm) for qm in _head_queries(q_ref, lo)]
        _store_heads(o_ref, _softmax_values(scores_t, vt[:, 0:n_keys]), lo)

    if first_tile == 0:
        @pl.when(t == 0)
        def _():
            attend(n_ctx)

    @pl.when(t > 0)
    def _():
        attend(k_ref.shape[1])


def _global_attn(q, k, v, n_ctx, need_ctx):
    bsz, tt, _ = q.shape
    first = 0 if need_ctx else n_ctx // TOK_TILE
    nt = tt // TOK_TILE - first
    kv = pl.BlockSpec((1, tt, LANES), lambda b, t: (b, 0, 0))
    return pl.pallas_call(
        functools.partial(_gattn_kernel, first_tile=first, n_ctx=n_ctx),
        grid=(bsz, nt),
        in_specs=[pl.BlockSpec((1, TOK_TILE, 256), lambda b, t: (b, t + first, 0)), kv, kv],
        out_specs=pl.BlockSpec((1, TOK_TILE, 256), lambda b, t: (b, t, 0)),
        out_shape=jax.ShapeDtypeStruct((bsz, nt * TOK_TILE, 256), BF16),
        scratch_shapes=[pltpu.VMEM((VT_ROWS, tt), BF16)],
        compiler_params=_params(("arbitrary", "arbitrary")),
    )(q, k, v)


def _wattn_kernel(q_ref, k_ref, v_ref, sink_ref, o_ref, vt, *, first_tile, n_ctx, n_lat):
    t = pl.program_id(1) + first_tile
    n_ctx_tiles = n_ctx // WQ_TILE
    lane = lax.broadcasted_iota(jnp.int32, (1, LANES), 1)
    lo = lane < HEAD_DIM
    kw = WQ_TILE + 2 * WINDOW
    sink = sink_ref[...]
    sinks = [sink[2 * hh + tile:2 * hh + tile + 1, 0:1] * LOG2E for tile in range(2) for hh in range(2)]

    @pl.when(pl.program_id(1) == 0)
    def _():
        _fill_vt(v_ref, vt)

    def attend(win):
        qms = _head_queries(q_ref, lo)
        if win:
            n = t - n_ctx_tiles
            ws = jnp.clip(n * WQ_TILE - WINDOW, 0, n_lat - kw)
            start = pl.multiple_of(n_ctx + ws, WINDOW)
            kk = jnp.concatenate([k_ref[0, 0:n_ctx, :], k_ref[0, pl.ds(start, kw), :]], axis=0)
            vv = jnp.concatenate([vt[:, 0:n_ctx], vt[:, pl.ds(start, kw)]], axis=1)
            row = lax.broadcasted_iota(jnp.int32, (n_ctx + kw, WQ_TILE), 0)
            qpos = n * WQ_TILE + lax.broadcasted_iota(jnp.int32, (n_ctx + kw, WQ_TILE), 1)
            valid = jnp.logical_or(row < n_ctx, jnp.abs(ws + row - n_ctx - qpos) <= WINDOW)
            scores_t = [lambda qm=qm: jnp.where(valid, _nt(kk, qm), NEG_BIG) for qm in qms]
        else:
            kk = k_ref[0, 0:n_ctx, :]
            vv = vt[:, 0:n_ctx]
            scores_t = [functools.partial(_nt, kk, qm) for qm in qms]
        _store_heads(o_ref, _softmax_values(scores_t, vv, sinks), lo)

    if first_tile == 0:
        @pl.when(t < n_ctx_tiles)
        def _():
            attend(False)

    @pl.when(t >= n_ctx_tiles)
    def _():
        attend(True)


def _window_attn(q, k, v, sink8, n_ctx, need_ctx):
    bsz, tt, _ = q.shape
    first = 0 if need_ctx else n_ctx // WQ_TILE
    nt = tt // WQ_TILE - first
    kv = pl.BlockSpec((1, tt, LANES), lambda b, t: (b, 0, 0))
    return pl.pallas_call(
        functools.partial(_wattn_kernel, first_tile=first, n_ctx=n_ctx, n_lat=tt - n_ctx),
        grid=(bsz, nt),
        in_specs=[pl.BlockSpec((1, WQ_TILE, 256), lambda b, t: (b, t + first, 0)), kv, kv,
                  pl.BlockSpec((8, LANES), lambda b, t: (0, 0))],
        out_specs=pl.BlockSpec((1, WQ_TILE, 256), lambda b, t: (b, t, 0)),
        out_shape=jax.ShapeDtypeStruct((bsz, nt * WQ_TILE, 256), BF16),
        scratch_shapes=[pltpu.VMEM((VT_ROWS, tt), BF16)],
        compiler_params=_params(("arbitrary", "arbitrary")),
    )(q, k, v, sink8)


def _outmlp_kernel(xc_ref, xl_ref, oa_ref, ob_ref, oc_ref, mod_ref, g_ref, wo_ref, w1_ref, w2_ref, o_ref, *,
                   first_tile):
    m = mod_ref[0]
    mix = _mm(oa_ref[0], wo_ref[0:A_Z, :])
    mix = mix + _mm(ob_ref[0], wo_ref[A_Z:A_Z + 256, :])
    mix = mix + _mm(oc_ref[0], wo_ref[A_Z + 256:A_Z + 512, :])
    x = _stream_tile(xc_ref, xl_ref, pl.program_id(1) + first_tile) + m[2:3] * mix
    h = _modulated(x, g_ref[...], m[3:4], m[4:5]).astype(BF16)
    fc = 1024
    acc = None
    for f in range(D_FF // fc):
        a = jnp.maximum(_mm(h, w1_ref[:, f * fc:(f + 1) * fc]), 0.0)
        y = _mm((a * a).astype(BF16), w2_ref[f * fc:(f + 1) * fc, :])
        acc = y if acc is None else acc + y
    o_ref[0] = x + m[5:6] * acc


def _out_mlp(ctx_arr, lat_arr, oa, ob, oc, mod, g, wo, w1, w2, n_ctx, need_ctx):
    bsz, tt, _ = oa.shape
    first = 0 if need_ctx else n_ctx // TOK_TILE
    nt = tt // TOK_TILE - first
    att = pl.BlockSpec((1, TOK_TILE, 256), lambda b, t: (b, t, 0))
    full = lambda r, c: pl.BlockSpec((r, c), lambda b, t: (0, 0))
    return pl.pallas_call(
        functools.partial(_outmlp_kernel, first_tile=first),
        grid=(bsz, nt),
        in_specs=_stream_specs(ctx_arr, lat_arr, first)
                 + [pl.BlockSpec((1, TOK_TILE, A_Z), lambda b, t: (b, t + first, 0)), att, att,
                    pl.BlockSpec((1, N_MOD, D_MODEL), lambda b, t: (_mod_row(b, t + first), 0, 0)),
                    full(1, D_MODEL), full(D_MODEL, D_MODEL), full(D_MODEL, D_FF), full(D_FF, D_MODEL)],
        out_specs=pl.BlockSpec((1, TOK_TILE, D_MODEL), lambda b, t: (b, t, 0)),
        out_shape=jax.ShapeDtypeStruct((bsz, nt * TOK_TILE, D_MODEL), F32),
        compiler_params=_params(("parallel", "arbitrary")),
    )(ctx_arr, lat_arr, oa, ob, oc, mod, g, wo, w1, w2)


_HEAD_PERM = (0, 2, 1, 3)


def _gate_param_lanes(p):
    out = jnp.zeros((LANES,), F32)
    idx, src = [], []
    for pair in range(GDN_HEADS // 2):
        for d in range(2):
            for j in range(2):
                idx.append(pair * 8 + 4 + d * 2 + j)
                src.append(d * GDN_HEADS + 2 * pair + j)
    return out.at[jnp.array(idx)].set(p.reshape(-1)[jnp.array(src)]).reshape(1, LANES)


def _rope_tables(n_ctx, n_lat):
    rows = n_lat // GRID_W
    row = jnp.repeat(jnp.arange(rows, dtype=F32), GRID_W)
    col = jnp.tile(jnp.arange(GRID_W, dtype=F32), rows)
    half = HEAD_DIM // 4
    inv_freq = ROPE_THETA ** (-jnp.arange(half, dtype=F32) / half)
    ang_r = row[:, None] * inv_freq
    ang_c = col[:, None] * inv_freq
    cr, sr, cc, sc = jnp.cos(ang_r), jnp.sin(ang_r), jnp.cos(ang_c), jnp.sin(ang_c)
    cos = jnp.concatenate([cr, cr, cc, cc], axis=-1)
    sin = jnp.concatenate([-sr, sr, -sc, sc], axis=-1)
    cos = jnp.concatenate([jnp.ones((n_ctx, HEAD_DIM), F32), cos], axis=0)
    sin = jnp.concatenate([jnp.zeros((n_ctx, HEAD_DIM), F32), sin], axis=0)
    return jnp.tile(cos, (1, 2)), jnp.tile(sin, (1, 2))


def _pad_rows(a, rows):
    return jnp.concatenate([a, jnp.zeros((rows - a.shape[0],) + a.shape[1:], a.dtype)], axis=0)


def kernel(x, c, ctx, c_ctx, w_mod, b_mod, g_attn, w_in, gdn_conv_w, gdn_a_log, gdn_dt_bias, gdn_norm_g,
           ga_q_norm_g, ga_k_norm_g, wa_q_norm_g, wa_k_norm_g, wa_sink, w_out, g_mlp, w_mlp_in, w_mlp_out):
    bsz, n_lat, _ = x.shape
    n_ctx = ctx.shape[1]
    depth = w_mod.shape[0]
    assert bsz <= 8 and n_ctx == TOK_TILE and n_lat % TOK_TILE == 0

    ctx_arr, lat_arr = ctx, x
    cond16 = _pad_rows(jnp.concatenate([_pad_rows(c, 8), c_ctx[None, :]], axis=0), 16)
    cos_t, sin_t = _rope_tables(n_ctx, n_lat)

    b0 = A_QKV + A_Z + 2 * A_GATES
    n_pairs = GDN_HEADS // 2

    def head_perm(a, axis):
        blocks = [lax.slice_in_dim(a, h * HEAD_DIM, (h + 1) * HEAD_DIM, axis=axis) for h in _HEAD_PERM]
        return jnp.concatenate(blocks, axis=axis)

    for l in range(depth):
        need_ctx = l < depth - 1
        mod = _ada_mod(cond16, w_mod[l], b_mod[l]).reshape(16, N_MOD, D_MODEL)
        w = w_in[l]
        wg = w[:, A_QKV + A_Z:b0].reshape(D_MODEL, 2, 2, n_pairs, 2).transpose(0, 3, 1, 2, 4).reshape(D_MODEL, -1)
        parts = [w[:, :A_QKV + A_Z], wg, jnp.zeros((D_MODEL, LANES - 2 * A_GATES), F32)]
        for base in (b0, b0 + B_QKV):
            parts += [head_perm(w[:, base:base + 256], 1), w[:, base + 256:base + B_QKV]]
        w_in_l = jnp.concatenate(parts, axis=1).astype(BF16)
        gains = _pad_rows(jnp.stack([jnp.tile(g, 2) for g in
                                     (ga_q_norm_g[l], ga_k_norm_g[l], wa_q_norm_g[l], wa_k_norm_g[l])]), 8)
        a_qkv, a_z, beta, gcum, gtot, bq, bk, bv, cq, ck, cv = _in_proj(
            ctx_arr, lat_arr, mod, g_attn[l].reshape(1, D_MODEL), w_in_l, _gate_param_lanes(gdn_a_log[l]),
            _gate_param_lanes(gdn_dt_bias[l]), cos_t, sin_t, gains)

        o_a = _gdn(a_qkv, beta, gcum, gtot, a_z, _pad_rows(gdn_conv_w[l], 8),
                   jnp.tile(gdn_norm_g[l], 2).reshape(1, LANES), n_ctx)

        o_b = _global_attn(bq, bk, bv, n_ctx, need_ctx)
        sink8 = _pad_rows(jnp.broadcast_to(wa_sink[l][:, None], (WA_HEADS, LANES)), 8)
        o_c = _window_attn(cq, ck, cv, sink8, n_ctx, need_ctx)

        wo = jnp.concatenate([w_out[l][:A_Z], head_perm(w_out[l][A_Z:A_Z + 256], 0),
                              head_perm(w_out[l][A_Z + 256:], 0)], axis=0).astype(BF16)
        xt = _out_mlp(ctx_arr, lat_arr, o_a, o_b, o_c, mod, g_mlp[l].reshape(1, D_MODEL), wo,
                      w_mlp_in[l].astype(BF16), w_mlp_out[l].astype(BF16), n_ctx, need_ctx)
        ctx_arr = lat_arr = xt
    return xt
```

```python
import functools

import jax
import jax.numpy as jnp
from jax import lax
from jax.experimental import pallas as pl
from jax.experimental.pallas import tpu as pltpu

F32 = jnp.float32
BF16 = jnp.bfloat16

D_MODEL = 1024
GRID_W = 64
HEAD_DIM = 64
ROPE_THETA = 10000.0
GDN_HEADS = 8
GDN_CONV = 5
CHUNK = 64
GA_HEADS = 4
WA_HEADS = 4
WINDOW = 128
D_FF = 4 * D_MODEL
N_MOD = 6
RMS_EPS = 1e-6
A_QKV = 1536
A_Z = 512
A_GATES = 16
B_QKV = 512
C_QKV = 512
LANES = 128
TOK_TILE = 256
WQ_TILE = 256
IN_SPB = 1
PREP_SLOTS = 6
PAR_GROUP = 9
PAR_STAGES = 14
NEG_BIG = -1e30
LOG2E = 1.4426950408889634
VMEM_LIMIT = 56 * 1024 * 1024


def _mm(a, b):
    return jnp.dot(a, b, preferred_element_type=F32)


def _nt(a, b):
    return lax.dot_general(a, b, (((1,), (1,)), ((), ())), preferred_element_type=F32)


def _sigmoid(x):
    return 1.0 / (1.0 + jnp.exp(-x))


def _params(sem):
    return pltpu.CompilerParams(dimension_semantics=sem, vmem_limit_bytes=VMEM_LIMIT)


def _mod_kernel(c_ref, w_ref, b_ref, o_ref):
    c = c_ref[...]
    s = c * _sigmoid(c)
    o_ref[...] = _mm(s.astype(BF16), w_ref[0].astype(BF16)) + b_ref[0]


def _ada_mod(cond16, w_mod, b_mod, layer):
    depth = w_mod.shape[0]
    n = N_MOD * D_MODEL
    tn = 1024
    return pl.pallas_call(
        _mod_kernel,
        grid=(n // tn,),
        in_specs=[pl.BlockSpec((16, D_MODEL), lambda j: (0, 0)),
                  pl.BlockSpec((1, D_MODEL, tn), lambda j: (layer, 0, j)),
                  pl.BlockSpec((1, 1, tn), lambda j: (layer, 0, j))],
        out_specs=pl.BlockSpec((16, tn), lambda j: (0, j)),
        out_shape=jax.ShapeDtypeStruct((16, n), F32),
        compiler_params=_params(("arbitrary",)),
    )(cond16, w_mod, b_mod.reshape(depth, 1, n))


def _mod_row(b, t):
    return jnp.where(t == 0, 8, b)


def _modulated(x, g, shift, scale):
    ms = jnp.mean(x * x, axis=-1, keepdims=True)
    y = x * lax.rsqrt(ms + RMS_EPS) * g
    return y * (1.0 + scale) + shift


IN_COLS = A_QKV + A_Z + LANES + B_QKV + C_QKV


def _pair_sumsq_rsqrt(y, lo, eps, denom):
    yy = y * y
    s0 = jnp.sum(jnp.where(lo, yy, 0.0), axis=-1, keepdims=True)
    s1 = jnp.sum(jnp.where(lo, 0.0, yy), axis=-1, keepdims=True)
    return jnp.where(lo, lax.rsqrt(s0 * (1.0 / denom) + eps), lax.rsqrt(s1 * (1.0 / denom) + eps))


def _pair_sumsq_rsqrt_mxu(y, ones_bd, eps, denom):
    yy = y * y
    hi = yy.astype(BF16)
    lo_part = (yy - hi.astype(F32)).astype(BF16)
    ss = _mm(hi, ones_bd) + _mm(lo_part, ones_bd)
    return lax.rsqrt(ss * (1.0 / denom) + eps)


def _norm_rope(t, gain, cos, sin, lo, swap_lo):
    y = t * _pair_sumsq_rsqrt(t, lo, RMS_EPS, float(HEAD_DIM)) * gain
    sw = jnp.where(swap_lo, pltpu.roll(y, LANES - 16, 1), pltpu.roll(y, 16, 1))
    return y * cos + sw * sin


def _stream_specs(ctx_arr, lat_arr, first=0):
    lat_off = 1 if lat_arr is ctx_arr else 0
    return [pl.BlockSpec((1, TOK_TILE, D_MODEL), lambda b, t: (b, 0, 0)),
            pl.BlockSpec((1, TOK_TILE, D_MODEL), lambda b, t: (b, jnp.maximum(t + first - 1, 0) + lat_off, 0))]


def _stream_tile(xc_ref, xl_ref, t):
    return jnp.where(t == 0, xc_ref[0], xl_ref[0])


def _inproj_kernel(xc_ref, xl_ref, mod_ref, mctx_ref, g_ref, w_ref, alog_ref, dtb_ref, cos_ref, sin_ref, gn_ref,
                   oa_ref, oz_ref, ob_ref, og_ref, ot_ref, bq_ref, bk_ref, bv_ref, cq_ref, ck_ref, cv_ref):
    is_ctx = pl.program_id(1) == 0
    g = g_ref[...]
    hs = []
    for i in range(IN_SPB):
        m = jnp.where(is_ctx, mctx_ref[0], mod_ref[i])
        x = jnp.where(is_ctx, xc_ref[i], xl_ref[i])
        hs.append(_modulated(x, g, m[0:1], m[1:2]))
    h = jnp.concatenate(hs, axis=0).astype(BF16)
    rows = [slice(i * TOK_TILE, (i + 1) * TOK_TILE) for i in range(IN_SPB)]
    lane = lax.broadcasted_iota(jnp.int32, (1, LANES), 1)
    lo = lane < HEAD_DIM
    acc = _mm(h, w_ref[:, 0:A_QKV])
    for i, r in enumerate(rows):
        oa_ref[i] = acc[r]
    off = A_QKV
    acc = _mm(h, w_ref[:, off:off + A_Z])
    for i, r in enumerate(rows):
        oz_ref[i] = acc[r]
    off += A_Z

    gt_all = _mm(h, w_ref[:, off:off + LANES])
    off += LANES
    p_alls = [_mm(h, w_ref[:, off + i * B_QKV:off + (i + 1) * B_QKV]) for i in range(2)]

    swap_lo = (lane % 32) < 16
    cos = cos_ref[...]
    sin = sin_ref[...]
    gn = gn_ref[...]
    scale = HEAD_DIM ** -0.5 * LOG2E
    for p_all, q_ref, k_ref, v_ref, gi in ((p_alls[0], bq_ref, bk_ref, bv_ref, 0),
                                           (p_alls[1], cq_ref, ck_ref, cv_ref, 2)):
        for i, r in enumerate(rows):
            p = p_all[r]
            for j in range(2):
                t = p[:, j * LANES:(j + 1) * LANES]
                q_ref[i, :, j * LANES:(j + 1) * LANES] = (
                    _norm_rope(t, gn[gi:gi + 1], cos, sin, lo, swap_lo) * scale).astype(BF16)
            k_ref[i] = _norm_rope(p[:, 2 * LANES:3 * LANES], gn[gi + 1:gi + 2], cos, sin, lo, swap_lo).astype(BF16)
            v_ref[i] = p[:, 3 * LANES:].astype(BF16)

    ri = lax.broadcasted_iota(jnp.int32, (TOK_TILE, TOK_TILE), 0)
    ci = lax.broadcasted_iota(jnp.int32, (TOK_TILE, TOK_TILE), 1)
    same = (ri // CHUNK) == (ci // CHUNK)
    blk = jnp.where(same, 1.0, 0.0).astype(BF16)
    tri = jnp.where(jnp.logical_and(same, ri >= ci), 1.0, 0.0).astype(BF16)
    for i, r in enumerate(rows):
        gt = gt_all[r]
        a = gt + dtb_ref[...]
        sp = jnp.maximum(a, 0.0) + jnp.log(1.0 + jnp.exp(-jnp.abs(a)))
        gl = -jnp.exp(alog_ref[...]) * sp
        g_hi = gl.astype(BF16)
        r1 = gl - g_hi.astype(F32)
        g_mid = r1.astype(BF16)
        g_lo = (r1 - g_mid.astype(F32)).astype(BF16)
        gpre = _mm(tri, g_hi) + _mm(tri, g_mid) + _mm(tri, g_lo)
        tot = _mm(blk, g_hi) + _mm(blk, g_mid) + _mm(blk, g_lo)
        ob_ref[i] = _sigmoid(gt)
        og_ref[i] = jnp.where((lane % 4) >= 2, tot - gpre + gl, gpre)
        ot_ref[i] = tot


def _in_proj(ctx_arr, lat_arr, mod, g, w, alog_l, dtb_l, cos_t, sin_t, gains):
    bsz = ctx_arr.shape[0]
    tt = cos_t.shape[0]
    nt = tt // TOK_TILE
    lat_off = 1 if lat_arr is ctx_arr else 0
    tile = lambda n: pl.BlockSpec((IN_SPB, TOK_TILE, n), lambda b, t: (b, t, 0))
    tab = pl.BlockSpec((TOK_TILE, LANES), lambda b, t: (t, 0))
    vec = pl.BlockSpec((1, LANES), lambda b, t: (0, 0))
    f32 = lambda n: jax.ShapeDtypeStruct((bsz, tt, n), F32)
    b16 = lambda n: jax.ShapeDtypeStruct((bsz, tt, n), BF16)
    return pl.pallas_call(
        _inproj_kernel,
        grid=(bsz // IN_SPB, nt),
        in_specs=[pl.BlockSpec((IN_SPB, TOK_TILE, D_MODEL), lambda b, t: (b, 0, 0)),
                  pl.BlockSpec((IN_SPB, TOK_TILE, D_MODEL), lambda b, t: (b, jnp.maximum(t - 1, 0) + lat_off, 0)),
                  pl.BlockSpec((IN_SPB, N_MOD, D_MODEL), lambda b, t: (b, 0, 0)),
                  pl.BlockSpec((1, N_MOD, D_MODEL), lambda b, t: (8, 0, 0)),
                  pl.BlockSpec((1, D_MODEL), lambda b, t: (0, 0)),
                  pl.BlockSpec((D_MODEL, IN_COLS), lambda b, t: (0, 0)), vec, vec,
                  tab, tab, pl.BlockSpec((8, LANES), lambda b, t: (0, 0))],
        out_specs=[tile(A_QKV), tile(A_Z), tile(LANES), tile(LANES), tile(LANES)]
                  + [tile(256), tile(LANES), tile(LANES)] * 2,
        out_shape=[f32(A_QKV), f32(A_Z), f32(LANES), f32(LANES), f32(LANES)]
                  + [b16(256), b16(LANES), b16(LANES)] * 2,
        compiler_params=_params(("parallel", "arbitrary")),
    )(ctx_arr, lat_arr, mod, mod, g, w, alog_l, dtb_l, cos_t, sin_t, gains)


def _gdn_kernel(q_ref, k_ref, v_ref, beta_ref, gc_ref, tot_ref, z_ref, cwq_ref, cwk_ref, cwv_ref, ng_ref, o_ref,
                qs, ks, vs, bs, gs, ts, grs, oacc, mq, bns, ext, *, n_ctx_chunks, n_chunks):
    hp = pl.program_id(1)
    tt = n_chunks * CHUNK
    lane = lax.broadcasted_iota(jnp.int32, (1, LANES), 1)
    lo = lane < HEAD_DIM
    shift = (LANES - 8 * hp) % LANES
    ri = lax.broadcasted_iota(jnp.int32, (LANES, LANES), 0)
    ci = lax.broadcasted_iota(jnp.int32, (LANES, LANES), 1)
    bd = (ri < HEAD_DIM) == (ci < HEAD_DIM)
    ones_bd = jnp.where(bd, 1.0, 0.0).astype(BF16)

    zeros8 = jnp.zeros((8, LANES), F32)

    def prep_chunk(c, slot):
        r0 = c * CHUNK
        first = c == 0 or c == n_ctx_chunks
        last = c == n_ctx_chunks - 1 or c == n_chunks - 1
        for i3, (src, cw_ref, dst, norm) in enumerate(((q_ref, cwq_ref, qs, True),
                                                       (k_ref, cwk_ref, ks, True),
                                                       (v_ref, cwv_ref, vs, False))):
            e = 3 * slot + i3
            ext[e, 0:8, :] = zeros8 if first else src[0, r0 - 8:r0, :]
            ext[e, 8:8 + CHUNK, :] = src[0, r0:r0 + CHUNK, :]
            ext[e, 8 + CHUNK:16 + CHUNK, :] = zeros8 if last else src[0, r0 + CHUNK:r0 + CHUNK + 8, :]
            cw = cw_ref[...]
            y = ext[e, 6:6 + CHUNK, :] * cw[0:1]
            for tap in range(1, GDN_CONV):
                y = y + ext[e, 6 + tap:6 + tap + CHUNK, :] * cw[tap:tap + 1]
            y = y * _sigmoid(y)
            if norm:
                y = y * _pair_sumsq_rsqrt_mxu(y, ones_bd, 1e-6, 1.0)
            dst[r0:r0 + CHUNK, :] = y
        bs[r0:r0 + CHUNK, :] = pltpu.roll(beta_ref[0, r0:r0 + CHUNK, :], shift, 1)
        gcr = pltpu.roll(gc_ref[0, r0:r0 + CHUNK, :], shift, 1)
        gs[r0:r0 + CHUNK, :] = gcr
        grs[c * 8:c * 8 + 8, :] = jnp.concatenate([gcr, gcr], axis=0).T[0:8, :]
        ts[c * 8:c * 8 + 8, :] = pltpu.roll(tot_ref[0, r0:r0 + 8, :], shift, 1)

    prep_count = [0]

    def prep_gen(chunks, per_round):
        for i, c in enumerate(chunks):
            prep_chunk(c, prep_count[0] % PREP_SLOTS)
            prep_count[0] += 1
            if i % per_round == per_round - 1:
                yield

    ti =lax.broadcasted_iota(jnp.int32, (CHUNK, LANES), 0)
    tj = lax.broadcasted_iota(jnp.int32, (CHUNK, LANES), 1) % CHUNK
    blk8 = (ti // 8) == (tj // 8)
    merge_masks = [jnp.logical_and((ti // (2 * m)) == (tj // (2 * m)), (ti // m) != (tj // m))
                   for m in (8, 16, 32)]
    masks = [((ti >= tj), (ti > tj)), ((ti <= tj), (ti < tj))]

    def blockdiag(x):
        xb = x.astype(BF16)
        return jnp.where(bd, jnp.concatenate([xb, xb], axis=0), jnp.zeros((LANES, LANES), BF16))

    def chunk_par(c, d):
        incl, strict = masks[d]
        r0 = c * CHUNK
        ib, ig = 2 * d, 4 + 2 * d
        bt = bs[pl.ds(r0, CHUNK), :]
        gt = gs[pl.ds(r0, CHUNK), :]
        bcol = jnp.where(lo, bt[:, ib:ib + 1], bt[:, ib + 1:ib + 2])
        gcol = jnp.where(lo, gt[:, ig:ig + 1], gt[:, ig + 1:ig + 2])
        k = ks[pl.ds(r0, CHUNK), :]
        q = qs[pl.ds(r0, CHUNK), :] * (HEAD_DIM ** -0.5)
        x = _nt(jnp.concatenate([k * bcol, q], axis=0).astype(BF16), blockdiag(k))
        yield
        gr8 = grs[c * 8:c * 8 + 8, :]
        grow = jnp.where(lo, gr8[ig:ig + 1, :], gr8[ig + 1:ig + 2, :])
        dec = jnp.exp(jnp.where(incl, gcol - grow, NEG_BIG))
        low = jnp.where(strict, x[:CHUNK] * dec, 0.0)
        intra = (x[CHUNK:] * dec).astype(BF16)
        l0 = jnp.where(blk8, low, 0.0)
        p = _mm(l0.astype(BF16), blockdiag(l0))
        yield
        n = -l0
        y = _mm(jnp.concatenate([n, p], axis=0).astype(BF16), blockdiag(p))
        yield
        n = n + p + y[:CHUNK]
        p = y[CHUNK:]
        y = _mm(n.astype(BF16), blockdiag(p))
        yield
        n = n + p + y
        for cmask in merge_masks:
            cm = jnp.where(cmask, low, 0.0)
            y = _mm(n.astype(BF16), blockdiag(cm))
            yield
            tc = cm + y
            y = _mm(tc.astype(BF16), blockdiag(n))
            yield
            n = n - tc - y
        vb = vs[pl.ds(r0, CHUNK), :] * bcol
        kbg = k * bcol * jnp.exp(gcol)
        y = _mm(n.astype(BF16), jnp.concatenate([blockdiag(vb), blockdiag(kbg)], axis=1))
        yield
        u = vb + y[:, :LANES]
        w = kbg + y[:, LANES:]
        bu, bw = blockdiag(u), blockdiag(w)
        iuw = _mm(intra, jnp.concatenate([bu, bw], axis=1))
        tl8 = ts[c * 8:c * 8 + 8, :]
        tl = jnp.where(lo, tl8[0:1, ig:ig + 1], tl8[0:1, ig + 1:ig + 2])
        ke = k * jnp.exp(tl - gcol)
        kett = jnp.concatenate([ke, ke], axis=0).T
        ket = jnp.where(lo, kett[:CHUNK], kett[CHUNK:]).astype(BF16)
        mb = _mm(ket, jnp.concatenate([bw, bu], axis=1))
        yield
        qe = q * jnp.exp(gcol) - iuw[:, LANES:]
        return jnp.concatenate([qe, mb[:, :LANES]], axis=0).astype(BF16), mb[:, LANES:], iuw[:, :LANES]

    def interleave(gens):
        res = [None] * len(gens)
        live = list(range(len(gens)))
        while live:
            for idx in list(live):
                try:
                    next(gens[idx])
                except StopIteration as stop:
                    res[idx] = stop.value
                    live.remove(idx)
        return res

    def scan_chunk(p, d):
        if d == 0:
            return p
        return n_ctx_chunks - 1 - p if p < n_ctx_chunks else n_chunks + n_ctx_chunks - 1 - p

    def seq_chain(g, d, s):
        ig = 4 + 2 * d
        for j in range(PAR_GROUP):
            c = scan_chunk(g * PAR_GROUP + j, d)
            r0 = c * CHUNK
            tl8 = ts[c * 8:c * 8 + 8, :]
            arow = jnp.exp(jnp.where(lo, tl8[0:1, ig:ig + 1], tl8[0:1, ig + 1:ig + 2]))
            y = _mm(mq[d, c], blockdiag(s))
            yield
            oacc[d, r0:r0 + CHUNK, :] = oacc[d, r0:r0 + CHUNK, :] + y[:CHUNK]
            s = arow * s - y[CHUNK:] + bns[d, c]
        return s

    ng = ng_ref[...]

    def fin_gen(chunks, per_round):
        for i, c in enumerate(chunks):
            r0 = c * CHUNK
            o = oacc[0, r0:r0 + CHUNK, :] + oacc[1, r0:r0 + CHUNK, :]
            z = z_ref[0, r0:r0 + CHUNK, :]
            y = o * _pair_sumsq_rsqrt_mxu(o, ones_bd, RMS_EPS, float(HEAD_DIM)) * ng
            o_ref[0, r0:r0 + CHUNK, :] = (y * (z * _sigmoid(z))).astype(o_ref.dtype)
            if i % per_round == per_round - 1:
                yield

    n_groups = n_chunks // PAR_GROUP
    group_chunks = [[(scan_chunk(g * PAR_GROUP + j, d), d) for j in range(PAR_GROUP) for d in range(2)]
                    for g in range(n_groups)]
    prepared = set()

    def to_prepare(g):
        todo = sorted(set(c for c, _ in group_chunks[g]) - prepared) if g < n_groups else []
        prepared.update(todo)
        return todo

    for _ in prep_gen(to_prepare(0), 1):
        pass
    states = None
    zero = jnp.zeros((CHUNK, LANES), F32)
    for g in range(n_groups + 1):
        cs = group_chunks[g] if g < n_groups else []
        gens = [chunk_par(c, d) for c, d in cs]
        n_par = len(gens)
        if g > 0:
            gens += [seq_chain(g - 1, d, (zero, zero)[d] if states is None else states[d]) for d in range(2)]
        nxt = to_prepare(g + 1)
        if nxt:
            gens.append(prep_gen(nxt, -(-len(nxt) // PAR_STAGES)))
        if g == n_groups:
            pending = set(c for c, _ in group_chunks[n_groups - 1])
            ready = [c for c in range(n_chunks) if c not in pending]
            gens.append(fin_gen(ready, -(-len(ready) // PAR_GROUP)))
        res = interleave(gens)
        for (c, d), (qm, bn, iu) in zip(cs, res[:n_par]):
            mq[d, c] = qm
            bns[d, c] = bn
            oacc[d, c * CHUNK:(c + 1) * CHUNK, :] = iu
        if g > 0:
            states = (res[n_par], res[n_par + 1])
    for _ in fin_gen(sorted(set(c for c, _ in group_chunks[n_groups - 1])), 1):
        pass


def _gdn(a_qkv, beta, gcum, gtot, a_z, cw8, ng2, n_ctx):
    bsz, tt, _ = a_qkv.shape
    n_pairs = GDN_HEADS // 2
    tok = lambda off: pl.BlockSpec((1, tt, LANES), lambda b, h: (b, 0, h + off))
    par = lambda off: pl.BlockSpec((8, LANES), lambda b, h: (0, h + off))
    gate = pl.BlockSpec((1, tt, LANES), lambda b, h: (b, 0, 0))
    kern = functools.partial(_gdn_kernel, n_ctx_chunks=n_ctx // CHUNK, n_chunks=tt // CHUNK)
    return pl.pallas_call(
        kern,
        grid=(bsz, n_pairs),
        in_specs=[tok(0), tok(n_pairs), tok(2 * n_pairs), gate, gate, gate, tok(0),
                  par(0), par(n_pairs), par(2 * n_pairs), pl.BlockSpec((1, LANES), lambda b, h: (0, 0))],
        out_specs=tok(0),
        out_shape=jax.ShapeDtypeStruct((bsz, tt, A_Z), BF16),
        scratch_shapes=[pltpu.VMEM((tt, LANES), F32)] * 5
                       + [pltpu.VMEM((tt // CHUNK * 8, LANES), F32)] * 2
                       + [pltpu.VMEM((2, tt, LANES), F32),
                          pltpu.VMEM((2, tt // CHUNK, 2 * CHUNK, LANES), BF16),
                          pltpu.VMEM((2, tt // CHUNK, CHUNK, LANES), F32),
                          pltpu.VMEM((3 * PREP_SLOTS, CHUNK + 16, LANES), F32)],
        compiler_params=_params(("parallel", "parallel")),
    )(a_qkv, a_qkv, a_qkv, beta, gcum, gtot, a_z, cw8, cw8, cw8, ng2)


VT_STEP = 256
VT_ROWS = LANES


def _head_queries(q_ref, lo):
    qms = []
    for tile in range(2):
        qt = q_ref[0, :, tile * LANES:(tile + 1) * LANES]
        for hh in range(2):
            qms.append(jnp.where(lo if hh == 0 else jnp.logical_not(lo), qt, jnp.zeros_like(qt)))
    return qms


def _fill_vt(v_ref, vt):
    for r in range(0, v_ref.shape[1], VT_STEP):
        vt[:, r:r + VT_STEP] = v_ref[0, r:r + VT_STEP, :].astype(F32).T.astype(BF16)


def _softmax_values(score_fns, vv_t, extra=None):
    scores = [fn() for fn in score_fns]
    probs, rinv = [], []
    for i, st in enumerate(scores):
        mx = jnp.max(st, axis=0, keepdims=True)
        if extra is not None:
            mx = jnp.maximum(mx, extra[i])
        p = jnp.exp2(st - mx)
        l = jnp.sum(p, axis=0, keepdims=True)
        if extra is not None:
            l = l + jnp.exp2(extra[i] - mx)
        rinv.append(1.0 / l)
        probs.append(p.astype(BF16))
    return [(_mm(vv_t, p) * r).T for p, r in zip(probs, rinv)]


def _store_heads(o_ref, outs, lo):
    for tile in range(2):
        o_ref[0, :, tile * LANES:(tile + 1) * LANES] = jnp.where(
            lo, outs[2 * tile], outs[2 * tile + 1]).astype(o_ref.dtype)


def _gattn_kernel(q_ref, k_ref, v_ref, o_ref, vt, *, first_tile, n_ctx):
    t = pl.program_id(1) + first_tile
    lane = lax.broadcasted_iota(jnp.int32, (1, LANES), 1)
    lo = lane < HEAD_DIM

    @pl.when(pl.program_id(1) == 0)
    def _():
        _fill_vt(v_ref, vt)

    def attend(n_keys):
        kk = k_ref[0, 0:n_keys, :]
        scores_t = [functools.partial(_nt, kk, qm) for qm in _head_queries(q_ref, lo)]
        _store_heads(o_ref, _softmax_values(scores_t, vt[:, 0:n_keys]), lo)

    if first_tile == 0:
        @pl.when(t == 0)
        def _():
            attend(n_ctx)

    @pl.when(t > 0)
    def _():
        attend(k_ref.shape[1])


def _global_attn(q, k, v, n_ctx, need_ctx):
    bsz, tt, _ = q.shape
    first = 0 if need_ctx else n_ctx // TOK_TILE
    nt = tt // TOK_TILE - first
    kv = pl.BlockSpec((1, tt, LANES), lambda b, t: (b, 0, 0))
    return pl.pallas_call(
        functools.partial(_gattn_kernel, first_tile=first, n_ctx=n_ctx),
        grid=(bsz, nt),
        in_specs=[pl.BlockSpec((1, TOK_TILE, 256), lambda b, t: (b, t + first, 0)), kv, kv],
        out_specs=pl.BlockSpec((1, TOK_TILE, 256), lambda b, t: (b, t, 0)),
        out_shape=jax.ShapeDtypeStruct((bsz, nt * TOK_TILE, 256), BF16),
        scratch_shapes=[pltpu.VMEM((VT_ROWS, tt), BF16)],
        compiler_params=_params(("arbitrary", "arbitrary")),
    )(q, k, v)


def _wattn_kernel(q_ref, k_ref, v_ref, sink_ref, o_ref, vt, *, first_tile, n_ctx, n_lat):
    t = pl.program_id(1) + first_tile
    n_ctx_tiles = n_ctx // WQ_TILE
    lane = lax.broadcasted_iota(jnp.int32, (1, LANES), 1)
    lo = lane < HEAD_DIM
    kw = WQ_TILE + 2 * WINDOW
    sink = sink_ref[...]
    sinks = [sink[2 * hh + tile:2 * hh + tile + 1, 0:1] * LOG2E for tile in range(2) for hh in range(2)]

    @pl.when(pl.program_id(1) == 0)
    def _():
        _fill_vt(v_ref, vt)

    def attend(win):
        qms = _head_queries(q_ref, lo)
        if win:
            n = t - n_ctx_tiles
            ws = jnp.clip(n * WQ_TILE - WINDOW, 0, n_lat - kw)
            start = pl.multiple_of(n_ctx + ws, WINDOW)
            kk = jnp.concatenate([k_ref[0, 0:n_ctx, :], k_ref[0, pl.ds(start, kw), :]], axis=0)
            vv = jnp.concatenate([vt[:, 0:n_ctx], vt[:, pl.ds(start, kw)]], axis=1)
            row = lax.broadcasted_iota(jnp.int32, (n_ctx + kw, WQ_TILE), 0)
            qpos = n * WQ_TILE + lax.broadcasted_iota(jnp.int32, (n_ctx + kw, WQ_TILE), 1)
            valid = jnp.logical_or(row < n_ctx, jnp.abs(ws + row - n_ctx - qpos) <= WINDOW)
            scores_t = [lambda qm=qm: jnp.where(valid, _nt(kk, qm), NEG_BIG) for qm in qms]
        else:
            kk = k_ref[0, 0:n_ctx, :]
            vv = vt[:, 0:n_ctx]
            scores_t = [functools.partial(_nt, kk, qm) for qm in qms]
        _store_heads(o_ref, _softmax_values(scores_t, vv, sinks), lo)

    if first_tile == 0:
        @pl.when(t < n_ctx_tiles)
        def _():
            attend(False)

    @pl.when(t >= n_ctx_tiles)
    def _():
        attend(True)


def _window_attn(q, k, v, sink8, n_ctx, need_ctx):
    bsz, tt, _ = q.shape
    first = 0 if need_ctx else n_ctx // WQ_TILE
    nt = tt // WQ_TILE - first
    kv = pl.BlockSpec((1, tt, LANES), lambda b, t: (b, 0, 0))
    return pl.pallas_call(
        functools.partial(_wattn_kernel, first_tile=first, n_ctx=n_ctx, n_lat=tt - n_ctx),
        grid=(bsz, nt),
        in_specs=[pl.BlockSpec((1, WQ_TILE, 256), lambda b, t: (b, t + first, 0)), kv, kv,
                  pl.BlockSpec((8, LANES), lambda b, t: (0, 0))],
        out_specs=pl.BlockSpec((1, WQ_TILE, 256), lambda b, t: (b, t, 0)),
        out_shape=jax.ShapeDtypeStruct((bsz, nt * WQ_TILE, 256), BF16),
        scratch_shapes=[pltpu.VMEM((VT_ROWS, tt), BF16)],
        compiler_params=_params(("arbitrary", "arbitrary")),
    )(q, k, v, sink8)


def _outmlp_kernel(xc_ref, xl_ref, oa_ref, ob_ref, oc_ref, mod_ref, g_ref, wo_ref, w1_ref, w2_ref, o_ref, *,
                   first_tile):
    m = mod_ref[0]
    mix = _mm(oa_ref[0], wo_ref[0:A_Z, :])
    mix = mix + _mm(ob_ref[0], wo_ref[A_Z:A_Z + 256, :])
    mix = mix + _mm(oc_ref[0], wo_ref[A_Z + 256:A_Z + 512, :])
    x = _stream_tile(xc_ref, xl_ref, pl.program_id(1) + first_tile) + m[2:3] * mix
    h = _modulated(x, g_ref[...], m[3:4], m[4:5]).astype(BF16)
    fc = 1024
    acc = None
    for f in range(D_FF // fc):
        a = jnp.maximum(_mm(h, w1_ref[:, f * fc:(f + 1) * fc]), 0.0)
        y = _mm((a * a).astype(BF16), w2_ref[f * fc:(f + 1) * fc, :])
        acc = y if acc is None else acc + y
    o_ref[0] = x + m[5:6] * acc


def _out_mlp(ctx_arr, lat_arr, oa, ob, oc, mod, g, wo, w1, w2, n_ctx, need_ctx):
    bsz, tt, _ = oa.shape
    first = 0 if need_ctx else n_ctx // TOK_TILE
    nt = tt // TOK_TILE - first
    att = pl.BlockSpec((1, TOK_TILE, 256), lambda b, t: (b, t, 0))
    full = lambda r, c: pl.BlockSpec((r, c), lambda b, t: (0, 0))
    return pl.pallas_call(
        functools.partial(_outmlp_kernel, first_tile=first),
        grid=(bsz, nt),
        in_specs=_stream_specs(ctx_arr, lat_arr, first)
                 + [pl.BlockSpec((1, TOK_TILE, A_Z), lambda b, t: (b, t + first, 0)), att, att,
                    pl.BlockSpec((1, N_MOD, D_MODEL), lambda b, t: (_mod_row(b, t + first), 0, 0)),
                    full(1, D_MODEL), full(D_MODEL, D_MODEL), full(D_MODEL, D_FF), full(D_FF, D_MODEL)],
        out_specs=pl.BlockSpec((1, TOK_TILE, D_MODEL), lambda b, t: (b, t, 0)),
        out_shape=jax.ShapeDtypeStruct((bsz, nt * TOK_TILE, D_MODEL), F32),
        compiler_params=_params(("parallel", "arbitrary")),
    )(ctx_arr, lat_arr, oa, ob, oc, mod, g, wo, w1, w2)


_HEAD_PERM = (0, 2, 1, 3)


def _gate_param_lanes(p):
    out = jnp.zeros((LANES,), F32)
    idx, src = [], []
    for pair in range(GDN_HEADS // 2):
        for d in range(2):
            for j in range(2):
                idx.append(pair * 8 + 4 + d * 2 + j)
                src.append(d * GDN_HEADS + 2 * pair + j)
    return out.at[jnp.array(idx)].set(p.reshape(-1)[jnp.array(src)]).reshape(1, LANES)


def _rope_tables(n_ctx, n_lat):
    rows = n_lat // GRID_W
    row = jnp.repeat(jnp.arange(rows, dtype=F32), GRID_W)
    col = jnp.tile(jnp.arange(GRID_W, dtype=F32), rows)
    half = HEAD_DIM // 4
    inv_freq = ROPE_THETA ** (-jnp.arange(half, dtype=F32) / half)
    ang_r = row[:, None] * inv_freq
    ang_c = col[:, None] * inv_freq
    cr, sr, cc, sc = jnp.cos(ang_r), jnp.sin(ang_r), jnp.cos(ang_c), jnp.sin(ang_c)
    cos = jnp.concatenate([cr, cr, cc, cc], axis=-1)
    sin = jnp.concatenate([-sr, sr, -sc, sc], axis=-1)
    cos = jnp.concatenate([jnp.ones((n_ctx, HEAD_DIM), F32), cos], axis=0)
    sin = jnp.concatenate([jnp.zeros((n_ctx, HEAD_DIM), F32), sin], axis=0)
    return jnp.tile(cos, (1, 2)), jnp.tile(sin, (1, 2))


def _pad_rows(a, rows):
    return jnp.concatenate([a, jnp.zeros((rows - a.shape[0],) + a.shape[1:], a.dtype)], axis=0)


def kernel(x, c, ctx, c_ctx, w_mod, b_mod, g_attn, w_in, gdn_conv_w, gdn_a_log, gdn_dt_bias, gdn_norm_g,
           ga_q_norm_g, ga_k_norm_g, wa_q_norm_g, wa_k_norm_g, wa_sink, w_out, g_mlp, w_mlp_in, w_mlp_out):
    bsz, n_lat, _ = x.shape
    n_ctx = ctx.shape[1]
    depth = w_mod.shape[0]
    assert bsz <= 8 and bsz % IN_SPB == 0 and n_ctx == TOK_TILE and n_lat % TOK_TILE == 0

    ctx_arr, lat_arr = ctx, x
    cond16 = _pad_rows(jnp.concatenate([_pad_rows(c, 8), c_ctx[None, :]], axis=0), 16)
    cos_t, sin_t = _rope_tables(n_ctx, n_lat)

    b0 = A_QKV + A_Z + 2 * A_GATES
    n_pairs = GDN_HEADS // 2

    def head_perm(a, axis):
        blocks = [lax.slice_in_dim(a, h * HEAD_DIM, (h + 1) * HEAD_DIM, axis=axis) for h in _HEAD_PERM]
        return jnp.concatenate(blocks, axis=axis)

    for l in range(depth):
        need_ctx = l < depth - 1
        mod = _ada_mod(cond16, w_mod, b_mod, l).reshape(16, N_MOD, D_MODEL)
        w = w_in[l]
        wg = w[:, A_QKV + A_Z:b0].reshape(D_MODEL, 2, 2, n_pairs, 2).transpose(0, 3, 1, 2, 4).reshape(D_MODEL, -1)
        parts = [w[:, :A_QKV + A_Z], wg, jnp.zeros((D_MODEL, LANES - 2 * A_GATES), F32)]
        for base in (b0, b0 + B_QKV):
            parts += [head_perm(w[:, base:base + 256], 1), w[:, base + 256:base + B_QKV]]
        w_in_l = jnp.concatenate(parts, axis=1).astype(BF16)
        gains = _pad_rows(jnp.stack([jnp.tile(g, 2) for g in
                                     (ga_q_norm_g[l], ga_k_norm_g[l], wa_q_norm_g[l], wa_k_norm_g[l])]), 8)
        a_qkv, a_z, beta, gcum, gtot, bq, bk, bv, cq, ck, cv = _in_proj(
            ctx_arr, lat_arr, mod, g_attn[l].reshape(1, D_MODEL), w_in_l, _gate_param_lanes(gdn_a_log[l]),
            _gate_param_lanes(gdn_dt_bias[l]), cos_t, sin_t, gains)

        o_a = _gdn(a_qkv, beta, gcum, gtot, a_z, _pad_rows(gdn_conv_w[l], 8),
                   jnp.tile(gdn_norm_g[l], 2).reshape(1, LANES), n_ctx)

        o_b = _global_attn(bq, bk, bv, n_ctx, need_ctx)
        sink8 = _pad_rows(jnp.broadcast_to(wa_sink[l][:, None], (WA_HEADS, LANES)), 8)
        o_c = _window_attn(cq, ck, cv, sink8, n_ctx, need_ctx)

        wo = jnp.concatenate([w_out[l][:A_Z], head_perm(w_out[l][A_Z:A_Z + 256], 0),
                              head_perm(w_out[l][A_Z + 256:], 0)], axis=0).astype(BF16)
        xt = _out_mlp(ctx_arr, lat_arr, o_a, o_b, o_c, mod, g_mlp[l].reshape(1, D_MODEL), wo,
                      w_mlp_in[l].astype(BF16), w_mlp_out[l].astype(BF16), n_ctx, need_ctx)
        ctx_arr = lat_arr = xt
    return xt
```

```python
import functools

import jax
import jax.numpy as jnp
from jax import lax
from jax.experimental import pallas as pl
from jax.experimental.pallas import tpu as pltpu

F32 = jnp.float32
BF16 = jnp.bfloat16

D_MODEL = 1024
GRID_W = 64
HEAD_DIM = 64
ROPE_THETA = 10000.0
GDN_HEADS = 8
GDN_CONV = 5
CHUNK = 64
GA_HEADS = 4
WA_HEADS = 4
WINDOW = 128
D_FF = 4 * D_MODEL
N_MOD = 6
RMS_EPS = 1e-6
A_QKV = 1536
A_Z = 512
A_GATES = 16
B_QKV = 512
C_QKV = 512
LANES = 128
TOK_TILE = 256
WQ_TILE = 256
IN_SPB = 1
MLP_FC = 1024
PREP_SLOTS = 6
PAR_GROUP = 6
PAR_STAGES = 14
NEG_BIG = -1e30
LOG2E = 1.4426950408889634
VMEM_LIMIT = 56 * 1024 * 1024


def _mm(a, b):
    return jnp.dot(a, b, preferred_element_type=F32)


def _nt(a, b):
    return lax.dot_general(a, b, (((1,), (1,)), ((), ())), preferred_element_type=F32)


def _sigmoid(x):
    return 1.0 / (1.0 + jnp.exp(-x))


def _params(sem):
    return pltpu.CompilerParams(dimension_semantics=sem, vmem_limit_bytes=VMEM_LIMIT)


def _mod_kernel(c_ref, w_ref, b_ref, o_ref):
    c = c_ref[...]
    s = c * _sigmoid(c)
    o_ref[...] = _mm(s.astype(BF16), w_ref[0].astype(BF16)) + b_ref[0]


def _ada_mod(cond16, w_mod, b_mod, layer):
    depth = w_mod.shape[0]
    n = N_MOD * D_MODEL
    tn = 1024
    return pl.pallas_call(
        _mod_kernel,
        grid=(n // tn,),
        in_specs=[pl.BlockSpec((16, D_MODEL), lambda j: (0, 0)),
                  pl.BlockSpec((1, D_MODEL, tn), lambda j: (layer, 0, j)),
                  pl.BlockSpec((1, 1, tn), lambda j: (layer, 0, j))],
        out_specs=pl.BlockSpec((16, tn), lambda j: (0, j)),
        out_shape=jax.ShapeDtypeStruct((16, n), F32),
        compiler_params=_params(("arbitrary",)),
    )(cond16, w_mod, b_mod.reshape(depth, 1, n))


def _mod_row(b, t):
    return jnp.where(t == 0, 8, b)


def _modulated(x, g, shift, scale):
    ms = jnp.mean(x * x, axis=-1, keepdims=True)
    y = x * lax.rsqrt(ms + RMS_EPS) * g
    return y * (1.0 + scale) + shift


IN_COLS = A_QKV + A_Z + LANES + B_QKV + C_QKV


def _pair_sumsq_rsqrt(y, lo, eps, denom):
    yy = y * y
    s0 = jnp.sum(jnp.where(lo, yy, 0.0), axis=-1, keepdims=True)
    s1 = jnp.sum(jnp.where(lo, 0.0, yy), axis=-1, keepdims=True)
    return jnp.where(lo, lax.rsqrt(s0 * (1.0 / denom) + eps), lax.rsqrt(s1 * (1.0 / denom) + eps))


def _pair_sumsq_rsqrt_mxu(y, ones_bd, eps, denom):
    yy = y * y
    hi = yy.astype(BF16)
    lo_part = (yy - hi.astype(F32)).astype(BF16)
    ss = _mm(hi, ones_bd) + _mm(lo_part, ones_bd)
    return lax.rsqrt(ss * (1.0 / denom) + eps)


def _norm_rope(t, gain, cos, sin, lo, swap_lo):
    y = t * _pair_sumsq_rsqrt(t, lo, RMS_EPS, float(HEAD_DIM)) * gain
    sw = jnp.where(swap_lo, pltpu.roll(y, LANES - 16, 1), pltpu.roll(y, 16, 1))
    return y * cos + sw * sin


def _stream_specs(ctx_arr, lat_arr, first=0):
    lat_off = 1 if lat_arr is ctx_arr else 0
    return [pl.BlockSpec((1, TOK_TILE, D_MODEL), lambda b, t: (b, 0, 0)),
            pl.BlockSpec((1, TOK_TILE, D_MODEL), lambda b, t: (b, jnp.maximum(t + first - 1, 0) + lat_off, 0))]


def _stream_tile(xc_ref, xl_ref, t):
    return jnp.where(t == 0, xc_ref[0], xl_ref[0])


def _inproj_kernel(xc_ref, xl_ref, mod_ref, mctx_ref, g_ref, w_ref, alog_ref, dtb_ref, cos_ref, sin_ref, gn_ref,
                   oa_ref, oz_ref, ob_ref, og_ref, ot_ref, bq_ref, bk_ref, bv_ref, cq_ref, ck_ref, cv_ref):
    is_ctx = pl.program_id(1) == 0
    g = g_ref[...]
    hs = []
    for i in range(IN_SPB):
        m = jnp.where(is_ctx, mctx_ref[0], mod_ref[i])
        x = jnp.where(is_ctx, xc_ref[i], xl_ref[i])
        hs.append(_modulated(x, g, m[0:1], m[1:2]))
    h = jnp.concatenate(hs, axis=0).astype(BF16)
    rows = [slice(i * TOK_TILE, (i + 1) * TOK_TILE) for i in range(IN_SPB)]
    lane = lax.broadcasted_iota(jnp.int32, (1, LANES), 1)
    lo = lane < HEAD_DIM
    acc = _mm(h, w_ref[:, 0:A_QKV])
    for i, r in enumerate(rows):
        oa_ref[i] = acc[r]
    off = A_QKV
    acc = _mm(h, w_ref[:, off:off + A_Z])
    for i, r in enumerate(rows):
        oz_ref[i] = acc[r]
    off += A_Z

    gt_all = _mm(h, w_ref[:, off:off + LANES])
    off += LANES
    p_alls = [_mm(h, w_ref[:, off + i * B_QKV:off + (i + 1) * B_QKV]) for i in range(2)]

    swap_lo = (lane % 32) < 16
    cos = cos_ref[...]
    sin = sin_ref[...]
    gn = gn_ref[...]
    scale = HEAD_DIM ** -0.5 * LOG2E
    for p_all, q_ref, k_ref, v_ref, gi in ((p_alls[0], bq_ref, bk_ref, bv_ref, 0),
                                           (p_alls[1], cq_ref, ck_ref, cv_ref, 2)):
        for i, r in enumerate(rows):
            p = p_all[r]
            for j in range(2):
                t = p[:, j * LANES:(j + 1) * LANES]
                q_ref[i, :, j * LANES:(j + 1) * LANES] = (
                    _norm_rope(t, gn[gi:gi + 1], cos, sin, lo, swap_lo) * scale).astype(BF16)
            k_ref[i] = _norm_rope(p[:, 2 * LANES:3 * LANES], gn[gi + 1:gi + 2], cos, sin, lo, swap_lo).astype(BF16)
            v_ref[i] = p[:, 3 * LANES:].astype(BF16)

    ri = lax.broadcasted_iota(jnp.int32, (TOK_TILE, TOK_TILE), 0)
    ci = lax.broadcasted_iota(jnp.int32, (TOK_TILE, TOK_TILE), 1)
    same = (ri // CHUNK) == (ci // CHUNK)
    blk = jnp.where(same, 1.0, 0.0).astype(BF16)
    tri = jnp.where(jnp.logical_and(same, ri >= ci), 1.0, 0.0).astype(BF16)
    for i, r in enumerate(rows):
        gt = gt_all[r]
        a = gt + dtb_ref[...]
        sp = jnp.maximum(a, 0.0) + jnp.log(1.0 + jnp.exp(-jnp.abs(a)))
        gl = -jnp.exp(alog_ref[...]) * sp
        g_hi = gl.astype(BF16)
        r1 = gl - g_hi.astype(F32)
        g_mid = r1.astype(BF16)
        g_lo = (r1 - g_mid.astype(F32)).astype(BF16)
        gpre = _mm(tri, g_hi) + _mm(tri, g_mid) + _mm(tri, g_lo)
        tot = _mm(blk, g_hi) + _mm(blk, g_mid) + _mm(blk, g_lo)
        ob_ref[i] = _sigmoid(gt)
        og_ref[i] = jnp.where((lane % 4) >= 2, tot - gpre + gl, gpre)
        ot_ref[i] = tot


def _in_proj(ctx_arr, lat_arr, mod, g, w, alog_l, dtb_l, cos_t, sin_t, gains):
    bsz = ctx_arr.shape[0]
    tt = cos_t.shape[0]
    nt = tt // TOK_TILE
    lat_off = 1 if lat_arr is ctx_arr else 0
    tile = lambda n: pl.BlockSpec((IN_SPB, TOK_TILE, n), lambda b, t: (b, t, 0))
    tab = pl.BlockSpec((TOK_TILE, LANES), lambda b, t: (t, 0))
    vec = pl.BlockSpec((1, LANES), lambda b, t: (0, 0))
    f32 = lambda n: jax.ShapeDtypeStruct((bsz, tt, n), F32)
    b16 = lambda n: jax.ShapeDtypeStruct((bsz, tt, n), BF16)
    return pl.pallas_call(
        _inproj_kernel,
        grid=(bsz // IN_SPB, nt),
        in_specs=[pl.BlockSpec((IN_SPB, TOK_TILE, D_MODEL), lambda b, t: (b, 0, 0)),
                  pl.BlockSpec((IN_SPB, TOK_TILE, D_MODEL), lambda b, t: (b, jnp.maximum(t - 1, 0) + lat_off, 0)),
                  pl.BlockSpec((IN_SPB, N_MOD, D_MODEL), lambda b, t: (b, 0, 0)),
                  pl.BlockSpec((1, N_MOD, D_MODEL), lambda b, t: (8, 0, 0)),
                  pl.BlockSpec((1, D_MODEL), lambda b, t: (0, 0)),
                  pl.BlockSpec((D_MODEL, IN_COLS), lambda b, t: (0, 0)), vec, vec,
                  tab, tab, pl.BlockSpec((8, LANES), lambda b, t: (0, 0))],
        out_specs=[tile(A_QKV), tile(A_Z), tile(LANES), tile(LANES), tile(LANES)]
                  + [tile(256), tile(LANES), tile(LANES)] * 2,
        out_shape=[f32(A_QKV), f32(A_Z), f32(LANES), f32(LANES), f32(LANES)]
                  + [b16(256), b16(LANES), b16(LANES)] * 2,
        compiler_params=_params(("parallel", "arbitrary")),
    )(ctx_arr, lat_arr, mod, mod, g, w, alog_l, dtb_l, cos_t, sin_t, gains)


def _gdn_kernel(q_ref, k_ref, v_ref, beta_ref, gc_ref, tot_ref, z_ref, cwq_ref, cwk_ref, cwv_ref, ng_ref, o_ref,
                qs, ks, vs, bs, gs, ts, grs, oacc, mq, bns, ext, *, n_ctx_chunks, n_chunks):
    hp = pl.program_id(1)
    tt = n_chunks * CHUNK
    lane = lax.broadcasted_iota(jnp.int32, (1, LANES), 1)
    lo = lane < HEAD_DIM
    shift = (LANES - 8 * hp) % LANES
    ri = lax.broadcasted_iota(jnp.int32, (LANES, LANES), 0)
    ci = lax.broadcasted_iota(jnp.int32, (LANES, LANES), 1)
    bd = (ri < HEAD_DIM) == (ci < HEAD_DIM)
    ones_bd = jnp.where(bd, 1.0, 0.0).astype(BF16)

    zeros8 = jnp.zeros((8, LANES), F32)

    def prep_chunk(c, slot):
        r0 = c * CHUNK
        first = c == 0 or c == n_ctx_chunks
        last = c == n_ctx_chunks - 1 or c == n_chunks - 1
        for i3, (src, cw_ref, dst, norm) in enumerate(((q_ref, cwq_ref, qs, True),
                                                       (k_ref, cwk_ref, ks, True),
                                                       (v_ref, cwv_ref, vs, False))):
            e = 3 * slot + i3
            ext[e, 0:8, :] = zeros8 if first else src[0, r0 - 8:r0, :]
            ext[e, 8:8 + CHUNK, :] = src[0, r0:r0 + CHUNK, :]
            ext[e, 8 + CHUNK:16 + CHUNK, :] = zeros8 if last else src[0, r0 + CHUNK:r0 + CHUNK + 8, :]
            cw = cw_ref[...]
            y = ext[e, 6:6 + CHUNK, :] * cw[0:1]
            for tap in range(1, GDN_CONV):
                y = y + ext[e, 6 + tap:6 + tap + CHUNK, :] * cw[tap:tap + 1]
            y = y * _sigmoid(y)
            if norm:
                y = y * _pair_sumsq_rsqrt_mxu(y, ones_bd, 1e-6, 1.0)
            dst[r0:r0 + CHUNK, :] = y
        bs[r0:r0 + CHUNK, :] = pltpu.roll(beta_ref[0, r0:r0 + CHUNK, :], shift, 1)
        gcr = pltpu.roll(gc_ref[0, r0:r0 + CHUNK, :], shift, 1)
        gs[r0:r0 + CHUNK, :] = gcr
        grs[c * 8:c * 8 + 8, :] = jnp.concatenate([gcr, gcr], axis=0).T[0:8, :]
        ts[c * 8:c * 8 + 8, :] = pltpu.roll(tot_ref[0, r0:r0 + 8, :], shift, 1)

    prep_count = [0]

    def prep_gen(chunks, per_round):
        for i, c in enumerate(chunks):
            prep_chunk(c, prep_count[0] % PREP_SLOTS)
            prep_count[0] += 1
            if i % per_round == per_round - 1:
                yield

    ti =lax.broadcasted_iota(jnp.int32, (CHUNK, LANES), 0)
    tj = lax.broadcasted_iota(jnp.int32, (CHUNK, LANES), 1) % CHUNK
    blk8 = (ti // 8) == (tj // 8)
    merge_masks = [jnp.logical_and((ti // (2 * m)) == (tj // (2 * m)), (ti // m) != (tj // m))
                   for m in (8, 16, 32)]
    masks = [((ti >= tj), (ti > tj)), ((ti <= tj), (ti < tj))]

    def blockdiag(x):
        xb = x.astype(BF16)
        return jnp.where(bd, jnp.concatenate([xb, xb], axis=0), jnp.zeros((LANES, LANES), BF16))

    def chunk_par(c, d):
        incl, strict = masks[d]
        r0 = c * CHUNK
        ib, ig = 2 * d, 4 + 2 * d
        bt = bs[pl.ds(r0, CHUNK), :]
        gt = gs[pl.ds(r0, CHUNK), :]
        bcol = jnp.where(lo, bt[:, ib:ib + 1], bt[:, ib + 1:ib + 2])
        gcol = jnp.where(lo, gt[:, ig:ig + 1], gt[:, ig + 1:ig + 2])
        k = ks[pl.ds(r0, CHUNK), :]
        q = qs[pl.ds(r0, CHUNK), :] * (HEAD_DIM ** -0.5)
        x = _nt(jnp.concatenate([k * bcol, q], axis=0).astype(BF16), blockdiag(k))
        yield
        gr8 = grs[c * 8:c * 8 + 8, :]
        grow = jnp.where(lo, gr8[ig:ig + 1, :], gr8[ig + 1:ig + 2, :])
        dec = jnp.exp(jnp.where(incl, gcol - grow, NEG_BIG))
        low = jnp.where(strict, x[:CHUNK] * dec, 0.0)
        intra = (x[CHUNK:] * dec).astype(BF16)
        l0 = jnp.where(blk8, low, 0.0)
        p = _mm(l0.astype(BF16), blockdiag(l0))
        yield
        n = -l0
        y = _mm(jnp.concatenate([n, p], axis=0).astype(BF16), blockdiag(p))
        yield
        n = n + p + y[:CHUNK]
        p = y[CHUNK:]
        y = _mm(n.astype(BF16), blockdiag(p))
        yield
        n = n + p + y
        for cmask in merge_masks:
            cm = jnp.where(cmask, low, 0.0)
            y = _mm(n.astype(BF16), blockdiag(cm))
            yield
            tc = cm + y
            y = _mm(tc.astype(BF16), blockdiag(n))
            yield
            n = n - tc - y
        vb = vs[pl.ds(r0, CHUNK), :] * bcol
        kbg = k * bcol * jnp.exp(gcol)
        y = _mm(n.astype(BF16), jnp.concatenate([blockdiag(vb), blockdiag(kbg)], axis=1))
        yield
        u = vb + y[:, :LANES]
        w = kbg + y[:, LANES:]
        bu, bw = blockdiag(u), blockdiag(w)
        iuw = _mm(intra, jnp.concatenate([bu, bw], axis=1))
        tl8 = ts[c * 8:c * 8 + 8, :]
        tl = jnp.where(lo, tl8[0:1, ig:ig + 1], tl8[0:1, ig + 1:ig + 2])
        ke = k * jnp.exp(tl - gcol)
        kett = jnp.concatenate([ke, ke], axis=0).T
        ket = jnp.where(lo, kett[:CHUNK], kett[CHUNK:]).astype(BF16)
        mb = _mm(ket, jnp.concatenate([bw, bu], axis=1))
        yield
        qe = q * jnp.exp(gcol) - iuw[:, LANES:]
        return jnp.concatenate([qe, mb[:, :LANES]], axis=0).astype(BF16), mb[:, LANES:], iuw[:, :LANES]

    def interleave(gens):
        res = [None] * len(gens)
        live = list(range(len(gens)))
        while live:
            for idx in list(live):
                try:
                    next(gens[idx])
                except StopIteration as stop:
                    res[idx] = stop.value
                    live.remove(idx)
        return res

    def scan_chunk(p, d):
        if d == 0:
            return p
        return n_ctx_chunks - 1 - p if p < n_ctx_chunks else n_chunks + n_ctx_chunks - 1 - p

    def seq_chain(g, d, s):
        ig = 4 + 2 * d
        for j in range(PAR_GROUP):
            c = scan_chunk(g * PAR_GROUP + j, d)
            r0 = c * CHUNK
            tl8 = ts[c * 8:c * 8 + 8, :]
            arow = jnp.exp(jnp.where(lo, tl8[0:1, ig:ig + 1], tl8[0:1, ig + 1:ig + 2]))
            y = _mm(mq[d, c], blockdiag(s))
            yield
            oacc[d, r0:r0 + CHUNK, :] = oacc[d, r0:r0 + CHUNK, :] + y[:CHUNK]
            s = arow * s - y[CHUNK:] + bns[d, c]
        return s

    ng = ng_ref[...]

    def fin_gen(chunks, per_round):
        for i, c in enumerate(chunks):
            r0 = c * CHUNK
            o = oacc[0, r0:r0 + CHUNK, :] + oacc[1, r0:r0 + CHUNK, :]
            z = z_ref[0, r0:r0 + CHUNK, :]
            y = o * _pair_sumsq_rsqrt_mxu(o, ones_bd, RMS_EPS, float(HEAD_DIM)) * ng
            o_ref[0, r0:r0 + CHUNK, :] = (y * (z * _sigmoid(z))).astype(o_ref.dtype)
            if i % per_round == per_round - 1:
                yield

    n_groups = n_chunks // PAR_GROUP
    group_chunks = [[(scan_chunk(g * PAR_GROUP + j, d), d) for j in range(PAR_GROUP) for d in range(2)]
                    for g in range(n_groups)]
    prepared = set()

    def to_prepare(g):
        todo = sorted(set(c for c, _ in group_chunks[g]) - prepared) if g < n_groups else []
        prepared.update(todo)
        return todo

    for _ in prep_gen(to_prepare(0), 1):
        pass
    states = None
    zero = jnp.zeros((CHUNK, LANES), F32)
    for g in range(n_groups + 1):
        cs = group_chunks[g] if g < n_groups else []
        gens = [chunk_par(c, d) for c, d in cs]
        n_par = len(gens)
        if g > 0:
            gens += [seq_chain(g - 1, d, (zero, zero)[d] if states is None else states[d]) for d in range(2)]
        nxt = to_prepare(g + 1)
        if nxt:
            gens.append(prep_gen(nxt, -(-len(nxt) // PAR_STAGES)))
        if g == n_groups:
            pending = set(c for c, _ in group_chunks[n_groups - 1])
            ready = [c for c in range(n_chunks) if c not in pending]
            gens.append(fin_gen(ready, -(-len(ready) // PAR_GROUP)))
        res = interleave(gens)
        for (c, d), (qm, bn, iu) in zip(cs, res[:n_par]):
            mq[d, c] = qm
            bns[d, c] = bn
            oacc[d, c * CHUNK:(c + 1) * CHUNK, :] = iu
        if g > 0:
            states = (res[n_par], res[n_par + 1])
    for _ in fin_gen(sorted(set(c for c, _ in group_chunks[n_groups - 1])), 1):
        pass


def _gdn(a_qkv, beta, gcum, gtot, a_z, cw8, ng2, n_ctx):
    bsz, tt, _ = a_qkv.shape
    n_pairs = GDN_HEADS // 2
    tok = lambda off: pl.BlockSpec((1, tt, LANES), lambda b, h: (b, 0, h + off))
    par = lambda off: pl.BlockSpec((8, LANES), lambda b, h: (0, h + off))
    gate = pl.BlockSpec((1, tt, LANES), lambda b, h: (b, 0, 0))
    kern = functools.partial(_gdn_kernel, n_ctx_chunks=n_ctx // CHUNK, n_chunks=tt // CHUNK)
    return pl.pallas_call(
        kern,
        grid=(bsz, n_pairs),
        in_specs=[tok(0), tok(n_pairs), tok(2 * n_pairs), gate, gate, gate, tok(0),
                  par(0), par(n_pairs), par(2 * n_pairs), pl.BlockSpec((1, LANES), lambda b, h: (0, 0))],
        out_specs=tok(0),
        out_shape=jax.ShapeDtypeStruct((bsz, tt, A_Z), BF16),
        scratch_shapes=[pltpu.VMEM((tt, LANES), F32)] * 5
                       + [pltpu.VMEM((tt // CHUNK * 8, LANES), F32)] * 2
                       + [pltpu.VMEM((2, tt, LANES), F32),
                          pltpu.VMEM((2, tt // CHUNK, 2 * CHUNK, LANES), BF16),
                          pltpu.VMEM((2, tt // CHUNK, CHUNK, LANES), F32),
                          pltpu.VMEM((3 * PREP_SLOTS, CHUNK + 16, LANES), F32)],
        compiler_params=_params(("parallel", "parallel")),
    )(a_qkv, a_qkv, a_qkv, beta, gcum, gtot, a_z, cw8, cw8, cw8, ng2)


VT_STEP = 256
VT_ROWS = LANES


def _head_queries(q_ref, lo):
    qms = []
    for tile in range(2):
        qt = q_ref[0, :, tile * LANES:(tile + 1) * LANES]
        for hh in range(2):
            qms.append(jnp.where(lo if hh == 0 else jnp.logical_not(lo), qt, jnp.zeros_like(qt)))
    return qms


def _fill_vt(v_ref, vt):
    for r in range(0, v_ref.shape[1], VT_STEP):
        vt[:, r:r + VT_STEP] = v_ref[0, r:r + VT_STEP, :].astype(F32).T.astype(BF16)


def _softmax_values(score_fns, vv_t, extra=None):
    scores = [fn() for fn in score_fns]
    probs, rinv = [], []
    for i, st in enumerate(scores):
        mx = jnp.max(st, axis=0, keepdims=True)
        if extra is not None:
            mx = jnp.maximum(mx, extra[i])
        p = jnp.exp2(st - mx)
        l = jnp.sum(p, axis=0, keepdims=True)
        if extra is not None:
            l = l + jnp.exp2(extra[i] - mx)
        rinv.append(1.0 / l)
        probs.append(p.astype(BF16))
    return [(_mm(vv_t, p) * r).T for p, r in zip(probs, rinv)]


def _store_heads(o_ref, outs, lo):
    for tile in range(2):
        o_ref[0, :, tile * LANES:(tile + 1) * LANES] = jnp.where(
            lo, outs[2 * tile], outs[2 * tile + 1]).astype(o_ref.dtype)


def _gattn_kernel(q_ref, k_ref, v_ref, o_ref, vt, *, first_tile, n_ctx):
    t = pl.program_id(1) + first_tile
    lane = lax.broadcasted_iota(jnp.int32, (1, LANES), 1)
    lo = lane < HEAD_DIM

    @pl.when(pl.program_id(1) == 0)
    def _():
        _fill_vt(v_ref, vt)

    def attend(n_keys):
        kk = k_ref[0, 0:n_keys, :]
        scores_t = [functools.partial(_nt, kk, qm) for qm in _head_queries(q_ref, lo)]
        _store_heads(o_ref, _softmax_values(scores_t, vt[:, 0:n_keys]), lo)

    if first_tile == 0:
        @pl.when(t == 0)
        def _():
            attend(n_ctx)

    @pl.when(t > 0)
    def _():
        attend(k_ref.shape[1])


def _global_attn(q, k, v, n_ctx, need_ctx):
    bsz, tt, _ = q.shape
    first = 0 if need_ctx else n_ctx // TOK_TILE
    nt = tt // TOK_TILE - first
    kv = pl.BlockSpec((1, tt, LANES), lambda b, t: (b, 0, 0))
    return pl.pallas_call(
        functools.partial(_gattn_kernel, first_tile=first, n_ctx=n_ctx),
        grid=(bsz, nt),
        in_specs=[pl.BlockSpec((1, TOK_TILE, 256), lambda b, t: (b, t + first, 0)), kv, kv],
        out_specs=pl.BlockSpec((1, TOK_TILE, 256), lambda b, t: (b, t, 0)),
        out_shape=jax.ShapeDtypeStruct((bsz, nt * TOK_TILE, 256), BF16),
        scratch_shapes=[pltpu.VMEM((VT_ROWS, tt), BF16)],
        compiler_params=_params(("arbitrary", "arbitrary")),
    )(q, k, v)


def _wattn_kernel(q_ref, k_ref, v_ref, sink_ref, o_ref, vt, *, first_tile, n_ctx, n_lat):
    t = pl.program_id(1) + first_tile
    n_ctx_tiles = n_ctx // WQ_TILE
    lane = lax.broadcasted_iota(jnp.int32, (1, LANES), 1)
    lo = lane < HEAD_DIM
    kw = WQ_TILE + 2 * WINDOW
    sink = sink_ref[...]
    sinks = [sink[2 * hh + tile:2 * hh + tile + 1, 0:1] * LOG2E for tile in range(2) for hh in range(2)]

    @pl.when(pl.program_id(1) == 0)
    def _():
        _fill_vt(v_ref, vt)

    def attend(win):
        qms = _head_queries(q_ref, lo)
        if win:
            n = t - n_ctx_tiles
            ws = jnp.clip(n * WQ_TILE - WINDOW, 0, n_lat - kw)
            start = pl.multiple_of(n_ctx + ws, WINDOW)
            kk = jnp.concatenate([k_ref[0, 0:n_ctx, :], k_ref[0, pl.ds(start, kw), :]], axis=0)
            vv = jnp.concatenate([vt[:, 0:n_ctx], vt[:, pl.ds(start, kw)]], axis=1)
            row = lax.broadcasted_iota(jnp.int32, (n_ctx + kw, WQ_TILE), 0)
            qpos = n * WQ_TILE + lax.broadcasted_iota(jnp.int32, (n_ctx + kw, WQ_TILE), 1)
            valid = jnp.logical_or(row < n_ctx, jnp.abs(ws + row - n_ctx - qpos) <= WINDOW)
            scores_t = [lambda qm=qm: jnp.where(valid, _nt(kk, qm), NEG_BIG) for qm in qms]
        else:
            kk = k_ref[0, 0:n_ctx, :]
            vv = vt[:, 0:n_ctx]
            scores_t = [functools.partial(_nt, kk, qm) for qm in qms]
        _store_heads(o_ref, _softmax_values(scores_t, vv, sinks), lo)

    if first_tile == 0:
        @pl.when(t < n_ctx_tiles)
        def _():
            attend(False)

    @pl.when(t >= n_ctx_tiles)
    def _():
        attend(True)


def _window_attn(q, k, v, sink8, n_ctx, need_ctx):
    bsz, tt, _ = q.shape
    first = 0 if need_ctx else n_ctx // WQ_TILE
    nt = tt // WQ_TILE - first
    kv = pl.BlockSpec((1, tt, LANES), lambda b, t: (b, 0, 0))
    return pl.pallas_call(
        functools.partial(_wattn_kernel, first_tile=first, n_ctx=n_ctx, n_lat=tt - n_ctx),
        grid=(bsz, nt),
        in_specs=[pl.BlockSpec((1, WQ_TILE, 256), lambda b, t: (b, t + first, 0)), kv, kv,
                  pl.BlockSpec((8, LANES), lambda b, t: (0, 0))],
        out_specs=pl.BlockSpec((1, WQ_TILE, 256), lambda b, t: (b, t, 0)),
        out_shape=jax.ShapeDtypeStruct((bsz, nt * WQ_TILE, 256), BF16),
        scratch_shapes=[pltpu.VMEM((VT_ROWS, tt), BF16)],
        compiler_params=_params(("arbitrary", "arbitrary")),
    )(q, k, v, sink8)


def _outmlp_kernel(xc_ref, xl_ref, oa_ref, ob_ref, oc_ref, mod_ref, g_ref, wo_ref, w1_ref, w2_ref, o_ref, *,
                   first_tile):
    m = mod_ref[0]
    mix = _mm(oa_ref[0], wo_ref[0:A_Z, :])
    mix = mix + _mm(ob_ref[0], wo_ref[A_Z:A_Z + 256, :])
    mix = mix + _mm(oc_ref[0], wo_ref[A_Z + 256:A_Z + 512, :])
    x = _stream_tile(xc_ref, xl_ref, pl.program_id(1) + first_tile) + m[2:3] * mix
    h = _modulated(x, g_ref[...], m[3:4], m[4:5]).astype(BF16)
    acc = None
    for f in range(D_FF // MLP_FC):
        a = jnp.maximum(_mm(h, w1_ref[:, f * MLP_FC:(f + 1) * MLP_FC]), 0.0)
        y = _mm((a * a).astype(BF16), w2_ref[f * MLP_FC:(f + 1) * MLP_FC, :])
        acc = y if acc is None else acc + y
    o_ref[0] = x + m[5:6] * acc


def _out_mlp(ctx_arr, lat_arr, oa, ob, oc, mod, g, wo, w1, w2, n_ctx, need_ctx):
    bsz, tt, _ = oa.shape
    first = 0 if need_ctx else n_ctx // TOK_TILE
    nt = tt // TOK_TILE - first
    att = pl.BlockSpec((1, TOK_TILE, 256), lambda b, t: (b, t, 0))
    full = lambda r, c: pl.BlockSpec((r, c), lambda b, t: (0, 0))
    return pl.pallas_call(
        functools.partial(_outmlp_kernel, first_tile=first),
        grid=(bsz, nt),
        in_specs=_stream_specs(ctx_arr, lat_arr, first)
                 + [pl.BlockSpec((1, TOK_TILE, A_Z), lambda b, t: (b, t + first, 0)), att, att,
                    pl.BlockSpec((1, N_MOD, D_MODEL), lambda b, t: (_mod_row(b, t + first), 0, 0)),
                    full(1, D_MODEL), full(D_MODEL, D_MODEL), full(D_MODEL, D_FF), full(D_FF, D_MODEL)],
        out_specs=pl.BlockSpec((1, TOK_TILE, D_MODEL), lambda b, t: (b, t, 0)),
        out_shape=jax.ShapeDtypeStruct((bsz, nt * TOK_TILE, D_MODEL), F32),
        compiler_params=_params(("parallel", "arbitrary")),
    )(ctx_arr, lat_arr, oa, ob, oc, mod, g, wo, w1, w2)


_HEAD_PERM = (0, 2, 1, 3)


def _gate_param_lanes(p):
    out = jnp.zeros((LANES,), F32)
    idx, src = [], []
    for pair in range(GDN_HEADS // 2):
        for d in range(2):
            for j in range(2):
                idx.append(pair * 8 + 4 + d * 2 + j)
                src.append(d * GDN_HEADS + 2 * pair + j)
    return out.at[jnp.array(idx)].set(p.reshape(-1)[jnp.array(src)]).reshape(1, LANES)


def _rope_tables(n_ctx, n_lat):
    rows = n_lat // GRID_W
    row = jnp.repeat(jnp.arange(rows, dtype=F32), GRID_W)
    col = jnp.tile(jnp.arange(GRID_W, dtype=F32), rows)
    half = HEAD_DIM // 4
    inv_freq = ROPE_THETA ** (-jnp.arange(half, dtype=F32) / half)
    ang_r = row[:, None] * inv_freq
    ang_c = col[:, None] * inv_freq
    cr, sr, cc, sc = jnp.cos(ang_r), jnp.sin(ang_r), jnp.cos(ang_c), jnp.sin(ang_c)
    cos = jnp.concatenate([cr, cr, cc, cc], axis=-1)
    sin = jnp.concatenate([-sr, sr, -sc, sc], axis=-1)
    cos = jnp.concatenate([jnp.ones((n_ctx, HEAD_DIM), F32), cos], axis=0)
    sin = jnp.concatenate([jnp.zeros((n_ctx, HEAD_DIM), F32), sin], axis=0)
    return jnp.tile(cos, (1, 2)), jnp.tile(sin, (1, 2))


def _pad_rows(a, rows):
    return jnp.concatenate([a, jnp.zeros((rows - a.shape[0],) + a.shape[1:], a.dtype)], axis=0)


def kernel(x, c, ctx, c_ctx, w_mod, b_mod, g_attn, w_in, gdn_conv_w, gdn_a_log, gdn_dt_bias, gdn_norm_g,
           ga_q_norm_g, ga_k_norm_g, wa_q_norm_g, wa_k_norm_g, wa_sink, w_out, g_mlp, w_mlp_in, w_mlp_out):
    bsz, n_lat, _ = x.shape
    n_ctx = ctx.shape[1]
    depth = w_mod.shape[0]
    assert bsz <= 8 and bsz % IN_SPB == 0 and n_ctx == TOK_TILE and n_lat % TOK_TILE == 0

    ctx_arr, lat_arr = ctx, x
    cond16 = _pad_rows(jnp.concatenate([_pad_rows(c, 8), c_ctx[None, :]], axis=0), 16)
    cos_t, sin_t = _rope_tables(n_ctx, n_lat)

    b0 = A_QKV + A_Z + 2 * A_GATES
    n_pairs = GDN_HEADS // 2

    def head_perm(a, axis):
        blocks = [lax.slice_in_dim(a, h * HEAD_DIM, (h + 1) * HEAD_DIM, axis=axis) for h in _HEAD_PERM]
        return jnp.concatenate(blocks, axis=axis)

    for l in range(depth):
        need_ctx = l < depth - 1
        mod = _ada_mod(cond16, w_mod, b_mod, l).reshape(16, N_MOD, D_MODEL)
        w = w_in[l]
        wg = w[:, A_QKV + A_Z:b0].reshape(D_MODEL, 2, 2, n_pairs, 2).transpose(0, 3, 1, 2, 4).reshape(D_MODEL, -1)
        parts = [w[:, :A_QKV + A_Z], wg, jnp.zeros((D_MODEL, LANES - 2 * A_GATES), F32)]
        for base in (b0, b0 + B_QKV):
            parts += [head_perm(w[:, base:base + 256], 1), w[:, base + 256:base + B_QKV]]
        w_in_l = jnp.concatenate(parts, axis=1).astype(BF16)
        gains = _pad_rows(jnp.stack([jnp.tile(g, 2) for g in
                                     (ga_q_norm_g[l], ga_k_norm_g[l], wa_q_norm_g[l], wa_k_norm_g[l])]), 8)
        a_qkv, a_z, beta, gcum, gtot, bq, bk, bv, cq, ck, cv = _in_proj(
            ctx_arr, lat_arr, mod, g_attn[l].reshape(1, D_MODEL), w_in_l, _gate_param_lanes(gdn_a_log[l]),
            _gate_param_lanes(gdn_dt_bias[l]), cos_t, sin_t, gains)

        o_a = _gdn(a_qkv, beta, gcum, gtot, a_z, _pad_rows(gdn_conv_w[l], 8),
                   jnp.tile(gdn_norm_g[l], 2).reshape(1, LANES), n_ctx)

        o_b = _global_attn(bq, bk, bv, n_ctx, need_ctx)
        sink8 = _pad_rows(jnp.broadcast_to(wa_sink[l][:, None], (WA_HEADS, LANES)), 8)
        o_c = _window_attn(cq, ck, cv, sink8, n_ctx, need_ctx)

        wo = jnp.concatenate([w_out[l][:A_Z], head_perm(w_out[l][A_Z:A_Z + 256], 0),
                              head_perm(w_out[l][A_Z + 256:], 0)], axis=0).astype(BF16)
        xt = _out_mlp(ctx_arr, lat_arr, o_a, o_b, o_c, mod, g_mlp[l].reshape(1, D_MODEL), wo,
                      w_mlp_in[l].astype(BF16), w_mlp_out[l].astype(BF16), n_ctx, need_ctx)
        ctx_arr = lat_arr = xt
    return xt
```
